```python
import jax, jax.numpy as jnp
from jax import lax
import numpy as np

D_MODEL = 1024
BATCH = 16
SEQ = 2048
DEPTH = 2

HEAD_DIM = 64
N_HEADS = D_MODEL // HEAD_DIM
HEADS_A = (N_HEADS * 3) // 8
HEADS_B = N_HEADS // 4
HEADS_C = N_HEADS - HEADS_A - HEADS_B
D_A = HEADS_A * HEAD_DIM
D_B = HEADS_B * HEAD_DIM
D_C = HEADS_C * HEAD_DIM
D_MIX = D_A + D_B + D_C
D_PROJ = 2 * D_A + 2 * D_B + 3 * D_C
CONV_A_WIDTH = 31
CONV_C_WIDTH = 3
CHUNK = 128
D_FF = 128 * ((8 * D_MODEL // 3 + 127) // 128)
RMS_EPS = 1e-6
LN_EPS = 1e-5
SPLITS = [int(s) for s in np.cumsum([D_A, D_A, D_B, D_B, D_C, D_C, D_C])[:-1]]

kernel_name = "hybrid_conformer_gmlp_shortconv_macaron"


def rms_norm(x, g):
    xf = x.astype(jnp.float32)
    y = xf * lax.rsqrt(jnp.mean(xf * xf, axis=-1, keepdims=True) + RMS_EPS)
    return (y * g.astype(jnp.float32)).astype(x.dtype)


def layer_norm(x, g, b):
    xf = x.astype(jnp.float32)
    mu = jnp.mean(xf, axis=-1, keepdims=True)
    xc = xf - mu
    var = jnp.mean(xc * xc, axis=-1, keepdims=True)
    y = xc * lax.rsqrt(var + LN_EPS) * g.astype(jnp.float32) + b.astype(jnp.float32)
    return y.astype(x.dtype)


def causal_depthwise_conv(x, w):
    k, c = w.shape
    return lax.conv_general_dilated(
        x, w[:, None, :].astype(x.dtype), window_strides=(1,), padding=[(k - 1, 0)],
        dimension_numbers=("NWC", "WIO", "NWC"), feature_group_count=c)


def swiglu_ffn(h, w_gate, w_up, w_down):
    return (jax.nn.silu(h @ w_gate) * (h @ w_up)) @ w_down


def chunked_spatial_gating(u, v, ln_g, ln_b, w_s, b_s):
    bsz, seq, _ = v.shape
    v = layer_norm(v, ln_g, ln_b)
    v = v.reshape(bsz, seq // CHUNK, CHUNK, HEADS_B, HEAD_DIM)
    mask = jnp.tril(jnp.ones((CHUNK, CHUNK), dtype=bool))
    w = jnp.where(mask, w_s, 0).astype(v.dtype)
    s = jnp.einsum("hts,bcshd->bcthd", w, v) + b_s.T.astype(v.dtype)[None, None, :, :, None]
    return u * s.reshape(bsz, seq, D_B)


def setup_inputs(seed: int = 0) -> dict:
    key = jax.random.key(seed)
    ks = jax.random.split(key, 24)
    f32 = jnp.float32

    def nrm(k, shape, scale):
        return jax.random.normal(k, shape, f32) * scale

    def gain(k, shape):
        return 1.0 + 0.02 * jax.random.normal(k, shape, f32)

    L = DEPTH
    return {
        "x": jax.random.normal(ks[0], (BATCH, SEQ, D_MODEL), f32),
        "ffn1_norm": gain(ks[1], (L, D_MODEL)),
        "ffn1_w_gate": nrm(ks[2], (L, D_MODEL, D_FF), D_MODEL ** -0.5),
        "ffn1_w_up": nrm(ks[3], (L, D_MODEL, D_FF), D_MODEL ** -0.5),
        "ffn1_w_down": nrm(ks[4], (L, D_FF, D_MODEL), D_FF ** -0.5),
        "mix_norm": gain(ks[5], (L, D_MODEL)),
        "w_in": nrm(ks[6], (L, D_MODEL, D_PROJ), D_MODEL ** -0.5),
        "conv_a_w": nrm(ks[7], (L, CONV_A_WIDTH, D_A), CONV_A_WIDTH ** -0.5),
        "conv_a_b": nrm(ks[8], (L, D_A), 0.02),
        "ln_a_g": gain(ks[9], (L, D_A)),
        "ln_a_b": nrm(ks[10], (L, D_A), 0.02),
        "ln_v_g": gain(ks[11], (L, D_B)),
        "ln_v_b": nrm(ks[12], (L, D_B), 0.02),
        "w_s": nrm(ks[13], (L, HEADS_B, CHUNK, CHUNK), CHUNK ** -0.5),
        "b_s": gain(ks[14], (L, HEADS_B, CHUNK)),
        "conv_c_w": nrm(ks[15], (L, CONV_C_WIDTH, D_C), CONV_C_WIDTH ** -0.5),
        "w_out": nrm(ks[16], (L, D_MIX, D_MODEL), D_MIX ** -0.5),
        "ffn2_norm": gain(ks[17], (L, D_MODEL)),
        "ffn2_w_gate": nrm(ks[18], (L, D_MODEL, D_FF), D_MODEL ** -0.5),
        "ffn2_w_up": nrm(ks[19], (L, D_MODEL, D_FF), D_MODEL ** -0.5),
        "ffn2_w_down": nrm(ks[20], (L, D_FF, D_MODEL), D_FF ** -0.5),
        "final_norm": gain(ks[21], (D_MODEL,)),
    }


def reference(x, ffn1_norm, ffn1_w_gate, ffn1_w_up, ffn1_w_down, mix_norm, w_in,
              conv_a_w, conv_a_b, ln_a_g, ln_a_b, ln_v_g, ln_v_b, w_s, b_s, conv_c_w,
              w_out, ffn2_norm, ffn2_w_gate, ffn2_w_up, ffn2_w_down, final_norm):
    for l in range(DEPTH):
        h = rms_norm(x, ffn1_norm[l])
        x = x + 0.5 * swiglu_ffn(h, ffn1_w_gate[l], ffn1_w_up[l], ffn1_w_down[l])

        h = rms_norm(x, mix_norm[l])
        p = h @ w_in[l]
        a_val, a_gate, b_u, b_v, c_b, c_c, c_x = jnp.split(p, SPLITS, axis=-1)

        a = a_val * jax.nn.sigmoid(a_gate)
        a = causal_depthwise_conv(a, conv_a_w[l]) + conv_a_b[l]
        a = jax.nn.silu(layer_norm(a, ln_a_g[l], ln_a_b[l]))

        b = chunked_spatial_gating(jax.nn.gelu(b_u), jax.nn.gelu(b_v),
                                   ln_v_g[l], ln_v_b[l], w_s[l], b_s[l])

        c = c_b * causal_depthwise_conv(c_c * c_x, conv_c_w[l])

        x = x + jnp.concatenate([a, b, c], axis=-1) @ w_out[l]

        h = rms_norm(x, ffn2_norm[l])
        x = x + 0.5 * swiglu_ffn(h, ffn2_w_gate[l], ffn2_w_up[l], ffn2_w_down[l])
    return rms_norm(x, final_norm)
```

```python
import functools

import jax
import jax.numpy as jnp
from jax import lax
from jax.experimental import pallas as pl
from jax.experimental.pallas import tpu as pltpu

RMS_EPS = 1e-6
LN_EPS = 1e-5

HEAD_DIM = 64
CHUNK = 128
CONV_A_WIDTH = 31
CONV_C_WIDTH = 3

V7X_SUBLANES = 8
V7X_MXU_COLS = 256
V7X_VMEM_LIMIT_BYTES = 56 * 1024 * 1024

FFN_ROWS = 512
MIX_ROWS = 512
A_HALO = 32
C_HALO = 8
CONV_ROW_BLOCK = 64

f32 = jnp.float32
bf16 = jnp.bfloat16


def _rms_norm(x, g):
    return x * lax.rsqrt(jnp.mean(x * x, axis=-1, keepdims=True) + RMS_EPS) * g


def _layer_norm(x, g, b):
    mu = jnp.mean(x, axis=-1, keepdims=True)
    xc = x - mu
    var = jnp.mean(xc * xc, axis=-1, keepdims=True)
    return xc * lax.rsqrt(var + LN_EPS) * g + b


def _dot(a, b):
    return jnp.dot(a, b, preferred_element_type=f32)


def _ffn_kernel(*refs, ff_chunks, final):
    if final:
        x_ref, g_ref, wg_ref, wu_ref, wd_ref, fn_ref, o_ref, h_ref, acc_ref = refs
    else:
        x_ref, g_ref, wg_ref, wu_ref, wd_ref, o_ref, h_ref, acc_ref = refs
    h_ref[...] = _rms_norm(x_ref[...], g_ref[...]).astype(bf16)
    start = 0
    for width in ff_chunks:
        cols = slice(start, start + width)
        h = h_ref[...]
        gate = _dot(h, wg_ref[:, cols])
        up = _dot(h, wu_ref[:, cols])
        act = (gate * jax.nn.sigmoid(gate) * up).astype(bf16)
        down = _dot(act, wd_ref[cols, :])
        if start == 0:
            acc_ref[...] = down
        else:
            acc_ref[...] += down
        start += width
    y = x_ref[...] + 0.5 * acc_ref[...]
    if final:
        y = _rms_norm(y, fn_ref[...])
    o_ref[...] = y


def _ff_chunks(d_ff):
    assert d_ff % V7X_MXU_COLS == 0
    return (V7X_MXU_COLS,) * (d_ff // V7X_MXU_COLS)


def _resident(shape):
    return pl.BlockSpec(shape, lambda *_: (0,) * len(shape), pipeline_mode=pl.Buffered(1))


def _ffn(x, norm_g, w_gate, w_up, w_down, final_g=None):
    n, d = x.shape
    d_ff = w_gate.shape[1]
    assert n % FFN_ROWS == 0
    final = final_g is not None
    row_spec = pl.BlockSpec((FFN_ROWS, d), lambda i: (i, 0))
    in_specs = [row_spec, _resident((1, d)), _resident((d, d_ff)), _resident((d, d_ff)),
                _resident((d_ff, d))]
    args = [x, norm_g.reshape(1, d), w_gate, w_up, w_down]
    if final:
        in_specs.append(_resident((1, d)))
        args.append(final_g.reshape(1, d))
    return pl.pallas_call(
        functools.partial(_ffn_kernel, ff_chunks=_ff_chunks(d_ff), final=final),
        out_shape=jax.ShapeDtypeStruct((n, d), f32),
        grid=(n // FFN_ROWS,),
        in_specs=in_specs,
        out_specs=row_spec,
        scratch_shapes=[pltpu.VMEM((FFN_ROWS, d), bf16), pltpu.VMEM((FFN_ROWS, d), f32)],
        compiler_params=pltpu.CompilerParams(
            dimension_semantics=("arbitrary",), vmem_limit_bytes=V7X_VMEM_LIMIT_BYTES),
        name="ffn_final" if final else "ffn",
    )(*args)


def _causal_conv(ext_ref, w_ref, halo, width, rows, bias):
    outs = []
    for r0 in range(0, rows, CONV_ROW_BLOCK):
        acc = None
        for k in range(width):
            src = halo + r0 - (width - 1) + k
            term = w_ref[k:k + 1, :] * ext_ref[src:src + CONV_ROW_BLOCK, :]
            acc = term if acc is None else acc + term
        outs.append(acc if bias is None else acc + bias)
    return outs


def _mixer_kernel(x_ref, g_ref, win_ref, caw_ref, cab_ref, lag_ref, lab_ref, lvg_ref, lvb_ref,
                  ws_ref, bsf_ref, ccw_ref, wout_ref, o_ref,
                  p_ref, aext_ref, cext_ref, mix_ref, wsm_ref, *, d_a, d_b, d_c):
    rows = x_ref.shape[1]
    heads_b = d_b // HEAD_DIM

    @pl.when(pl.program_id(1) == 0)
    def _():
        aext_ref[0:A_HALO, :] = jnp.zeros((A_HALO, d_a), f32)
        cext_ref[0:C_HALO, :] = jnp.zeros((C_HALO, d_c), f32)
        t_idx = lax.broadcasted_iota(jnp.int32, (CHUNK, CHUNK), 0)
        s_idx = lax.broadcasted_iota(jnp.int32, (CHUNK, CHUNK), 1)
        for h in range(heads_b):
            wsm_ref[h * CHUNK:(h + 1) * CHUNK, :] = jnp.where(
                s_idx <= t_idx, ws_ref[h], 0.0).astype(bf16)

    h = _rms_norm(x_ref[0], g_ref[...]).astype(bf16)
    p_ref[...] = _dot(h, win_ref[...])

    o_av, o_ag, o_bu, o_bv = 0, d_a, 2 * d_a, 2 * d_a + d_b
    o_cb = 2 * d_a + 2 * d_b
    o_cc, o_cx = o_cb + d_c, o_cb + 2 * d_c

    aext_ref[A_HALO:A_HALO + rows, :] = (
        p_ref[:, o_av:o_av + d_a] * jax.nn.sigmoid(p_ref[:, o_ag:o_ag + d_a]))
    conv_a = _causal_conv(aext_ref, caw_ref, A_HALO, CONV_A_WIDTH, rows, cab_ref[...])
    for i, blk in enumerate(conv_a):
        r0 = i * CONV_ROW_BLOCK
        y = _layer_norm(blk, lag_ref[...], lab_ref[...])
        mix_ref[r0:r0 + CONV_ROW_BLOCK, 0:d_a] = (y * jax.nn.sigmoid(y)).astype(bf16)
    aext_ref[0:A_HALO, :] = aext_ref[rows:rows + A_HALO, :]

    lane_head = lax.broadcasted_iota(jnp.int32, (CHUNK, d_b), 1) // HEAD_DIM
    for c0 in range(0, rows, CHUNK):
        u = jax.nn.gelu(p_ref[c0:c0 + CHUNK, o_bu:o_bu + d_b])
        v = jax.nn.gelu(p_ref[c0:c0 + CHUNK, o_bv:o_bv + d_b])
        v = _layer_norm(v, lvg_ref[...], lvb_ref[...]).astype(bf16)
        r = _dot(wsm_ref[...], v)
        s = r[0:CHUNK]
        for hd in range(1, heads_b):
            s = jnp.where(lane_head == hd, r[hd * CHUNK:(hd + 1) * CHUNK], s)
        mix_ref[c0:c0 + CHUNK, d_a:d_a + d_b] = (u * (s + bsf_ref[...])).astype(bf16)

    cext_ref[C_HALO:C_HALO + rows, :] = p_ref[:, o_cc:o_cc + d_c] * p_ref[:, o_cx:o_cx + d_c]
    conv_c = _causal_conv(cext_ref, ccw_ref, C_HALO, CONV_C_WIDTH, rows, None)
    for i, blk in enumerate(conv_c):
        r0 = i * CONV_ROW_BLOCK
        mix_ref[r0:r0 + CONV_ROW_BLOCK, d_a + d_b:d_a + d_b + d_c] = (
            p_ref[r0:r0 + CONV_ROW_BLOCK, o_cb:o_cb + d_c] * blk).astype(bf16)
    cext_ref[0:C_HALO, :] = cext_ref[rows:rows + C_HALO, :]

    o_ref[0] = x_ref[0] + _dot(mix_ref[...], wout_ref[...])


def _mixer(x, norm_g, w_in, conv_a_w, conv_a_b, ln_a_g, ln_a_b, ln_v_g, ln_v_b, w_s, b_s,
           conv_c_w, w_out):
    bsz, seq, d = x.shape
    d_a = conv_a_w.shape[1]
    d_b = ln_v_g.shape[0]
    d_c = conv_c_w.shape[1]
    heads_b = d_b // HEAD_DIM
    d_proj = w_in.shape[1]
    assert seq % MIX_ROWS == 0 and MIX_ROWS % CHUNK == 0 and MIX_ROWS % CONV_ROW_BLOCK == 0
    assert w_s.shape == (heads_b, CHUNK, CHUNK) and d_proj == 2 * d_a + 2 * d_b + 3 * d_c
    bs_full = jnp.repeat(b_s.T, HEAD_DIM, axis=1)
    row_spec = pl.BlockSpec((1, MIX_ROWS, d), lambda b, t: (b, t, 0))
    vec = lambda a: a.reshape(1, -1)
    args = [x, vec(norm_g), w_in, conv_a_w, vec(conv_a_b), vec(ln_a_g), vec(ln_a_b),
            vec(ln_v_g), vec(ln_v_b), w_s, bs_full, conv_c_w, w_out]
    in_specs = [row_spec] + [_resident(a.shape) for a in args[1:]]
    return pl.pallas_call(
        functools.partial(_mixer_kernel, d_a=d_a, d_b=d_b, d_c=d_c),
        out_shape=jax.ShapeDtypeStruct((bsz, seq, d), f32),
        grid=(bsz, seq // MIX_ROWS),
        in_specs=in_specs,
        out_specs=row_spec,
        scratch_shapes=[
            pltpu.VMEM((MIX_ROWS, d_proj), f32),
            pltpu.VMEM((A_HALO + MIX_ROWS, d_a), f32),
            pltpu.VMEM((C_HALO + MIX_ROWS, d_c), f32),
            pltpu.VMEM((MIX_ROWS, d_a + d_b + d_c), bf16),
            pltpu.VMEM((heads_b * CHUNK, CHUNK), bf16),
        ],
        compiler_params=pltpu.CompilerParams(
            dimension_semantics=("arbitrary", "arbitrary"),
            vmem_limit_bytes=V7X_VMEM_LIMIT_BYTES),
        name="mixer",
    )(*args)


def kernel(x, ffn1_norm, ffn1_w_gate, ffn1_w_up, ffn1_w_down, mix_norm, w_in, conv_a_w, conv_a_b,
           ln_a_g, ln_a_b, ln_v_g, ln_v_b, w_s, b_s, conv_c_w, w_out, ffn2_norm, ffn2_w_gate,
           ffn2_w_up, ffn2_w_down, final_norm):
    bsz, seq, d = x.shape
    depth = w_in.shape[0]
    xf = x.reshape(bsz * seq, d)
    for l in range(depth):
        xf = _ffn(xf, ffn1_norm[l], ffn1_w_gate[l].astype(bf16), ffn1_w_up[l].astype(bf16),
                  ffn1_w_down[l].astype(bf16))
        xm = _mixer(xf.reshape(bsz, seq, d), mix_norm[l], w_in[l].astype(bf16), conv_a_w[l],
                    conv_a_b[l], ln_a_g[l], ln_a_b[l], ln_v_g[l], ln_v_b[l], w_s[l], b_s[l],
                    conv_c_w[l], w_out[l].astype(bf16))
        xf = _ffn(xm.reshape(bsz * seq, d), ffn2_norm[l], ffn2_w_gate[l].astype(bf16),
                  ffn2_w_up[l].astype(bf16), ffn2_w_down[l].astype(bf16),
                  final_g=final_norm if l == depth - 1 else None)
    return xf.reshape(bsz, seq, d)
```

```python
import functools

import jax
import jax.numpy as jnp
from jax import lax
from jax.experimental import pallas as pl
from jax.experimental.pallas import tpu as pltpu

RMS_EPS = 1e-6
LN_EPS = 1e-5

HEAD_DIM = 64
CHUNK = 128
CONV_A_WIDTH = 31
CONV_C_WIDTH = 3

V7X_SUBLANES = 8
V7X_MXU_COLS = 256
V7X_VMEM_LIMIT_BYTES = 56 * 1024 * 1024

FFN_ROWS = 512
MIX_ROWS = 512
A_HALO = 32
C_HALO = 8
CONV_ROW_BLOCK = 64

f32 = jnp.float32
bf16 = jnp.bfloat16


def _rms_norm(x, g):
    return x * lax.rsqrt(jnp.mean(x * x, axis=-1, keepdims=True) + RMS_EPS) * g


def _layer_norm(x, g, b):
    mu = jnp.mean(x, axis=-1, keepdims=True)
    xc = x - mu
    var = jnp.mean(xc * xc, axis=-1, keepdims=True)
    return xc * lax.rsqrt(var + LN_EPS) * g + b


def _dot(a, b):
    return jnp.dot(a, b, preferred_element_type=f32)


def _ffn_kernel(*refs, ff_chunks, final):
    if final:
        x_ref, g_ref, wg_ref, wu_ref, wd_ref, fn_ref, o_ref, h_ref, acc_ref = refs
    else:
        x_ref, g_ref, wg_ref, wu_ref, wd_ref, o_ref, h_ref, acc_ref = refs
    h_ref[...] = _rms_norm(x_ref[...], g_ref[...]).astype(bf16)
    start = 0
    for width in ff_chunks:
        cols = slice(start, start + width)
        h = h_ref[...]
        gate = _dot(h, wg_ref[:, cols])
        up = _dot(h, wu_ref[:, cols])
        act = (gate * jax.nn.sigmoid(gate) * up).astype(bf16)
        down = _dot(act, wd_ref[cols, :])
        if start == 0:
            acc_ref[...] = down
        else:
            acc_ref[...] += down
        start += width
    y = x_ref[...] + 0.5 * acc_ref[...]
    if final:
        y = _rms_norm(y, fn_ref[...])
    o_ref[...] = y


def _ff_chunks(d_ff):
    assert d_ff % V7X_MXU_COLS == 0
    return (V7X_MXU_COLS,) * (d_ff // V7X_MXU_COLS)


def _resident(shape):
    return pl.BlockSpec(shape, lambda *_: (0,) * len(shape), pipeline_mode=pl.Buffered(1))


def _resident_layer(stacked, layer):
    tail = stacked.shape[1:]
    return pl.BlockSpec((None,) + tail, lambda *_: (layer,) + (0,) * len(tail),
                        pipeline_mode=pl.Buffered(1))


def _ffn(x, norm_g, w_gate, w_up, w_down, layer, final_g=None):
    n, d = x.shape
    d_ff = w_gate.shape[2]
    assert n % FFN_ROWS == 0
    final = final_g is not None
    row_spec = pl.BlockSpec((FFN_ROWS, d), lambda i: (i, 0))
    in_specs = [row_spec, _resident((1, d)), _resident_layer(w_gate, layer),
                _resident_layer(w_up, layer), _resident_layer(w_down, layer)]
    args = [x, norm_g.reshape(1, d), w_gate, w_up, w_down]
    if final:
        in_specs.append(_resident((1, d)))
        args.append(final_g.reshape(1, d))
    return pl.pallas_call(
        functools.partial(_ffn_kernel, ff_chunks=_ff_chunks(d_ff), final=final),
        out_shape=jax.ShapeDtypeStruct((n, d), f32),
        grid=(n // FFN_ROWS,),
        in_specs=in_specs,
        out_specs=row_spec,
        scratch_shapes=[pltpu.VMEM((FFN_ROWS, d), bf16), pltpu.VMEM((FFN_ROWS, d), f32)],
        compiler_params=pltpu.CompilerParams(
            dimension_semantics=("arbitrary",), vmem_limit_bytes=V7X_VMEM_LIMIT_BYTES),
        name="ffn_final" if final else "ffn",
    )(*args)


def _causal_conv(ext_ref, w_ref, halo, width, rows, bias):
    first = halo - (width - 1)
    classes = sorted({(first + k) % V7X_SUBLANES for k in range(width)})
    outs = []
    for r0 in range(0, rows, CONV_ROW_BLOCK):
        acc = bias
        for cls in classes:
            part = None
            for k in range(width):
                src = first + r0 + k
                if src % V7X_SUBLANES != cls:
                    continue
                term = w_ref[k:k + 1, :] * ext_ref[src:src + CONV_ROW_BLOCK, :]
                part = term if part is None else part + term
            acc = part if acc is None else acc + part
        outs.append(acc)
    return outs


def _mixer_kernel(x_ref, g_ref, win_ref, caw_ref, cab_ref, lag_ref, lab_ref, lvg_ref, lvb_ref,
                  ws_ref, bsf_ref, ccw_ref, wout_ref, o_ref,
                  p_ref, aext_ref, cext_ref, mix_ref, wsm_ref, *, d_a, d_b, d_c):
    rows = x_ref.shape[1]
    heads_b = d_b // HEAD_DIM

    @pl.when(pl.program_id(1) == 0)
    def _():
        aext_ref[0:A_HALO, :] = jnp.zeros((A_HALO, d_a), f32)
        cext_ref[0:C_HALO, :] = jnp.zeros((C_HALO, d_c), f32)
        t_idx = lax.broadcasted_iota(jnp.int32, (CHUNK, CHUNK), 0)
        s_idx = lax.broadcasted_iota(jnp.int32, (CHUNK, CHUNK), 1)
        for h in range(heads_b):
            wsm_ref[h * CHUNK:(h + 1) * CHUNK, :] = jnp.where(
                s_idx <= t_idx, ws_ref[h], 0.0).astype(bf16)

    h = _rms_norm(x_ref[0], g_ref[...]).astype(bf16)
    p_ref[...] = _dot(h, win_ref[...])

    o_av, o_ag, o_bu, o_bv = 0, d_a, 2 * d_a, 2 * d_a + d_b
    o_cb = 2 * d_a + 2 * d_b
    o_cc, o_cx = o_cb + d_c, o_cb + 2 * d_c

    aext_ref[A_HALO:A_HALO + rows, :] = (
        p_ref[:, o_av:o_av + d_a] * jax.nn.sigmoid(p_ref[:, o_ag:o_ag + d_a]))
    conv_a = _causal_conv(aext_ref, caw_ref, A_HALO, CONV_A_WIDTH, rows, cab_ref[...])
    for i, blk in enumerate(conv_a):
        r0 = i * CONV_ROW_BLOCK
        y = _layer_norm(blk, lag_ref[...], lab_ref[...])
        mix_ref[r0:r0 + CONV_ROW_BLOCK, 0:d_a] = (y * jax.nn.sigmoid(y)).astype(bf16)
    aext_ref[0:A_HALO, :] = aext_ref[rows:rows + A_HALO, :]

    lane_head = lax.broadcasted_iota(jnp.int32, (CHUNK, d_b), 1) // HEAD_DIM
    for c0 in range(0, rows, CHUNK):
        u = jax.nn.gelu(p_ref[c0:c0 + CHUNK, o_bu:o_bu + d_b])
        v = jax.nn.gelu(p_ref[c0:c0 + CHUNK, o_bv:o_bv + d_b])
        v = _layer_norm(v, lvg_ref[...], lvb_ref[...]).astype(bf16)
        r = _dot(wsm_ref[...], v)
        s = r[0:CHUNK]
        for hd in range(1, heads_b):
            s = jnp.where(lane_head == hd, r[hd * CHUNK:(hd + 1) * CHUNK], s)
        mix_ref[c0:c0 + CHUNK, d_a:d_a + d_b] = (u * (s + bsf_ref[...])).astype(bf16)

    cext_ref[C_HALO:C_HALO + rows, :] = p_ref[:, o_cc:o_cc + d_c] * p_ref[:, o_cx:o_cx + d_c]
    conv_c = _causal_conv(cext_ref, ccw_ref, C_HALO, CONV_C_WIDTH, rows, None)
    for i, blk in enumerate(conv_c):
        r0 = i * CONV_ROW_BLOCK
        mix_ref[r0:r0 + CONV_ROW_BLOCK, d_a + d_b:d_a + d_b + d_c] = (
            p_ref[r0:r0 + CONV_ROW_BLOCK, o_cb:o_cb + d_c] * blk).astype(bf16)
    cext_ref[0:C_HALO, :] = cext_ref[rows:rows + C_HALO, :]

    o_ref[0] = x_ref[0] + _dot(mix_ref[...], wout_ref[...])


def _mixer(x, norm_g, w_in, conv_a_w, conv_a_b, ln_a_g, ln_a_b, ln_v_g, ln_v_b, w_s, b_s,
           conv_c_w, w_out, layer):
    bsz, seq, d = x.shape
    d_a = conv_a_w.shape[1]
    d_b = ln_v_g.shape[0]
    d_c = conv_c_w.shape[1]
    heads_b = d_b // HEAD_DIM
    d_proj = w_in.shape[2]
    assert seq % MIX_ROWS == 0 and MIX_ROWS % CHUNK == 0 and MIX_ROWS % CONV_ROW_BLOCK == 0
    assert w_s.shape == (heads_b, CHUNK, CHUNK) and d_proj == 2 * d_a + 2 * d_b + 3 * d_c
    bs_full = jnp.repeat(b_s.T, HEAD_DIM, axis=1)
    row_spec = pl.BlockSpec((1, MIX_ROWS, d), lambda b, t: (b, t, 0))
    vec = lambda a: a.reshape(1, -1)
    args = [x, vec(norm_g), w_in, conv_a_w, vec(conv_a_b), vec(ln_a_g), vec(ln_a_b),
            vec(ln_v_g), vec(ln_v_b), w_s, bs_full, conv_c_w, w_out]
    in_specs = [row_spec] + [
        _resident_layer(a, layer) if a is w_in or a is w_out else _resident(a.shape)
        for a in args[1:]]
    return pl.pallas_call(
        functools.partial(_mixer_kernel, d_a=d_a, d_b=d_b, d_c=d_c),
        out_shape=jax.ShapeDtypeStruct((bsz, seq, d), f32),
        grid=(bsz, seq // MIX_ROWS),
        in_specs=in_specs,
        out_specs=row_spec,
        scratch_shapes=[
            pltpu.VMEM((MIX_ROWS, d_proj), f32),
            pltpu.VMEM((A_HALO + MIX_ROWS, d_a), f32),
            pltpu.VMEM((C_HALO + MIX_ROWS, d_c), f32),
            pltpu.VMEM((MIX_ROWS, d_a + d_b + d_c), bf16),
            pltpu.VMEM((heads_b * CHUNK, CHUNK), bf16),
        ],
        compiler_params=pltpu.CompilerParams(
            dimension_semantics=("arbitrary", "arbitrary"),
            vmem_limit_bytes=V7X_VMEM_LIMIT_BYTES),
        name="mixer",
    )(*args)


def kernel(x, ffn1_norm, ffn1_w_gate, ffn1_w_up, ffn1_w_down, mix_norm, w_in, conv_a_w, conv_a_b,
           ln_a_g, ln_a_b, ln_v_g, ln_v_b, w_s, b_s, conv_c_w, w_out, ffn2_norm, ffn2_w_gate,
           ffn2_w_up, ffn2_w_down, final_norm):
    bsz, seq, d = x.shape
    depth = w_in.shape[0]
    xf = x.reshape(bsz * seq, d)
    ffn1_w = [w.astype(bf16) for w in (ffn1_w_gate, ffn1_w_up, ffn1_w_down)]
    ffn2_w = [w.astype(bf16) for w in (ffn2_w_gate, ffn2_w_up, ffn2_w_down)]
    w_in_bf, w_out_bf = w_in.astype(bf16), w_out.astype(bf16)
    for l in range(depth):
        xf = _ffn(xf, ffn1_norm[l], *ffn1_w, layer=l)
        xm = _mixer(xf.reshape(bsz, seq, d), mix_norm[l], w_in_bf, conv_a_w[l], conv_a_b[l],
                    ln_a_g[l], ln_a_b[l], ln_v_g[l], ln_v_b[l], w_s[l], b_s[l], conv_c_w[l],
                    w_out_bf, layer=l)
        xf = _ffn(xm.reshape(bsz * seq, d), ffn2_norm[l], *ffn2_w, layer=l,
                  final_g=final_norm if l == depth - 1 else None)
    return xf.reshape(bsz, seq, d)
```

```python
import functools

import jax
import jax.numpy as jnp
from jax import lax
from jax.experimental import pallas as pl
from jax.experimental.pallas import tpu as pltpu

RMS_EPS = 1e-6
LN_EPS = 1e-5

HEAD_DIM = 64
CHUNK = 128
CONV_A_WIDTH = 31
CONV_C_WIDTH = 3

V7X_SUBLANES = 8
V7X_MXU_COLS = 256
V7X_VMEM_LIMIT_BYTES = 56 * 1024 * 1024

FFN_ROWS = 1024
MIX_ROWS = 1024
A_HALO = 32
C_HALO = 8
CONV_ROW_BLOCK = 64

f32 = jnp.float32
bf16 = jnp.bfloat16


def _rms_norm(x, g):
    return x * lax.rsqrt(jnp.mean(x * x, axis=-1, keepdims=True) + RMS_EPS) * g


def _layer_norm(x, g, b):
    mu = jnp.mean(x, axis=-1, keepdims=True)
    xc = x - mu
    var = jnp.mean(xc * xc, axis=-1, keepdims=True)
    return xc * lax.rsqrt(var + LN_EPS) * g + b


def _dot(a, b):
    return jnp.dot(a, b, preferred_element_type=f32)


def _ffn_kernel(*refs, ff_chunks, final):
    if final:
        x_ref, g_ref, wg_ref, wu_ref, wd_ref, fn_ref, o_ref, h_ref, acc_ref = refs
    else:
        x_ref, g_ref, wg_ref, wu_ref, wd_ref, o_ref, h_ref, acc_ref = refs
    h_ref[...] = _rms_norm(x_ref[...], g_ref[...]).astype(bf16)
    start = 0
    for width in ff_chunks:
        cols = slice(start, start + width)
        h = h_ref[...]
        gate = _dot(h, wg_ref[:, cols])
        up = _dot(h, wu_ref[:, cols])
        act = (gate * jax.nn.sigmoid(gate) * up).astype(bf16)
        down = _dot(act, wd_ref[cols, :])
        if start == 0:
            acc_ref[...] = down
        else:
            acc_ref[...] += down
        start += width
    y = x_ref[...] + 0.5 * acc_ref[...]
    if final:
        y = _rms_norm(y, fn_ref[...])
    o_ref[...] = y


def _ff_chunks(d_ff):
    assert d_ff % V7X_MXU_COLS == 0
    return (V7X_MXU_COLS,) * (d_ff // V7X_MXU_COLS)


def _resident(shape):
    return pl.BlockSpec(shape, lambda *_: (0,) * len(shape), pipeline_mode=pl.Buffered(1))


def _resident_layer(stacked, layer):
    tail = stacked.shape[1:]
    return pl.BlockSpec((None,) + tail, lambda *_: (layer,) + (0,) * len(tail),
                        pipeline_mode=pl.Buffered(1))


def _ffn(x, norm_g, w_gate, w_up, w_down, layer, final_g=None):
    n, d = x.shape
    d_ff = w_gate.shape[2]
    assert n % FFN_ROWS == 0
    final = final_g is not None
    row_spec = pl.BlockSpec((FFN_ROWS, d), lambda i: (i, 0))
    in_specs = [row_spec, _resident((1, d)), _resident_layer(w_gate, layer),
                _resident_layer(w_up, layer), _resident_layer(w_down, layer)]
    args = [x, norm_g.reshape(1, d), w_gate, w_up, w_down]
    if final:
        in_specs.append(_resident((1, d)))
        args.append(final_g.reshape(1, d))
    return pl.pallas_call(
        functools.partial(_ffn_kernel, ff_chunks=_ff_chunks(d_ff), final=final),
        out_shape=jax.ShapeDtypeStruct((n, d), f32),
        grid=(n // FFN_ROWS,),
        in_specs=in_specs,
        out_specs=row_spec,
        scratch_shapes=[pltpu.VMEM((FFN_ROWS, d), bf16), pltpu.VMEM((FFN_ROWS, d), f32)],
        compiler_params=pltpu.CompilerParams(
            dimension_semantics=("arbitrary",), vmem_limit_bytes=V7X_VMEM_LIMIT_BYTES),
        name="ffn_final" if final else "ffn",
    )(*args)


def _causal_conv(ext_ref, w_ref, halo, width, rows, bias):
    first = halo - (width - 1)
    classes = sorted({(first + k) % V7X_SUBLANES for k in range(width)})
    outs = []
    for r0 in range(0, rows, CONV_ROW_BLOCK):
        acc = bias
        for cls in classes:
            part = None
            for k in range(width):
                src = first + r0 + k
                if src % V7X_SUBLANES != cls:
                    continue
                term = w_ref[k:k + 1, :] * ext_ref[src:src + CONV_ROW_BLOCK, :]
                part = term if part is None else part + term
            acc = part if acc is None else acc + part
        outs.append(acc)
    return outs


def _mixer_kernel(x_ref, g_ref, win_ref, caw_ref, cab_ref, lag_ref, lab_ref, lvg_ref, lvb_ref,
                  ws_ref, bsf_ref, ccw_ref, wout_ref, o_ref,
                  p_ref, aext_ref, cext_ref, mix_ref, wsm_ref, *, d_a, d_b, d_c):
    rows = x_ref.shape[1]
    heads_b = d_b // HEAD_DIM

    @pl.when(pl.program_id(1) == 0)
    def _():
        aext_ref[0:A_HALO, :] = jnp.zeros((A_HALO, d_a), f32)
        cext_ref[0:C_HALO, :] = jnp.zeros((C_HALO, d_c), f32)
        t_idx = lax.broadcasted_iota(jnp.int32, (CHUNK, CHUNK), 0)
        s_idx = lax.broadcasted_iota(jnp.int32, (CHUNK, CHUNK), 1)
        for h in range(heads_b):
            wsm_ref[h * CHUNK:(h + 1) * CHUNK, :] = jnp.where(
                s_idx <= t_idx, ws_ref[h], 0.0).astype(bf16)

    h = _rms_norm(x_ref[0], g_ref[...]).astype(bf16)
    p_ref[...] = _dot(h, win_ref[...])

    o_av, o_ag, o_bu, o_bv = 0, d_a, 2 * d_a, 2 * d_a + d_b
    o_cb = 2 * d_a + 2 * d_b
    o_cc, o_cx = o_cb + d_c, o_cb + 2 * d_c

    aext_ref[A_HALO:A_HALO + rows, :] = (
        p_ref[:, o_av:o_av + d_a] * jax.nn.sigmoid(p_ref[:, o_ag:o_ag + d_a]))
    conv_a = _causal_conv(aext_ref, caw_ref, A_HALO, CONV_A_WIDTH, rows, cab_ref[...])
    for i, blk in enumerate(conv_a):
        r0 = i * CONV_ROW_BLOCK
        y = _layer_norm(blk, lag_ref[...], lab_ref[...])
        mix_ref[r0:r0 + CONV_ROW_BLOCK, 0:d_a] = (y * jax.nn.sigmoid(y)).astype(bf16)
    aext_ref[0:A_HALO, :] = aext_ref[rows:rows + A_HALO, :]

    lane_head = lax.broadcasted_iota(jnp.int32, (CHUNK, d_b), 1) // HEAD_DIM
    for c0 in range(0, rows, CHUNK):
        u = jax.nn.gelu(p_ref[c0:c0 + CHUNK, o_bu:o_bu + d_b])
        v = jax.nn.gelu(p_ref[c0:c0 + CHUNK, o_bv:o_bv + d_b])
        v = _layer_norm(v, lvg_ref[...], lvb_ref[...]).astype(bf16)
        r = _dot(wsm_ref[...], v)
        s = r[0:CHUNK]
        for hd in range(1, heads_b):
            s = jnp.where(lane_head == hd, r[hd * CHUNK:(hd + 1) * CHUNK], s)
        mix_ref[c0:c0 + CHUNK, d_a:d_a + d_b] = (u * (s + bsf_ref[...])).astype(bf16)

    cext_ref[C_HALO:C_HALO + rows, :] = p_ref[:, o_cc:o_cc + d_c] * p_ref[:, o_cx:o_cx + d_c]
    conv_c = _causal_conv(cext_ref, ccw_ref, C_HALO, CONV_C_WIDTH, rows, None)
    for i, blk in enumerate(conv_c):
        r0 = i * CONV_ROW_BLOCK
        mix_ref[r0:r0 + CONV_ROW_BLOCK, d_a + d_b:d_a + d_b + d_c] = (
            p_ref[r0:r0 + CONV_ROW_BLOCK, o_cb:o_cb + d_c] * blk).astype(bf16)
    cext_ref[0:C_HALO, :] = cext_ref[rows:rows + C_HALO, :]

    o_ref[0] = x_ref[0] + _dot(mix_ref[...], wout_ref[...])


def _mixer(x, norm_g, w_in, conv_a_w, conv_a_b, ln_a_g, ln_a_b, ln_v_g, ln_v_b, w_s, b_s,
           conv_c_w, w_out, layer):
    bsz, seq, d = x.shape
    d_a = conv_a_w.shape[1]
    d_b = ln_v_g.shape[0]
    d_c = conv_c_w.shape[1]
    heads_b = d_b // HEAD_DIM
    d_proj = w_in.shape[2]
    assert seq % MIX_ROWS == 0 and MIX_ROWS % CHUNK == 0 and MIX_ROWS % CONV_ROW_BLOCK == 0
    assert w_s.shape == (heads_b, CHUNK, CHUNK) and d_proj == 2 * d_a + 2 * d_b + 3 * d_c
    bs_full = jnp.repeat(b_s.T, HEAD_DIM, axis=1)
    row_spec = pl.BlockSpec((1, MIX_ROWS, d), lambda b, t: (b, t, 0))
    vec = lambda a: a.reshape(1, -1)
    args = [x, vec(norm_g), w_in, conv_a_w, vec(conv_a_b), vec(ln_a_g), vec(ln_a_b),
            vec(ln_v_g), vec(ln_v_b), w_s, bs_full, conv_c_w, w_out]
    in_specs = [row_spec] + [
        _resident_layer(a, layer) if a is w_in or a is w_out else _resident(a.shape)
        for a in args[1:]]
    return pl.pallas_call(
        functools.partial(_mixer_kernel, d_a=d_a, d_b=d_b, d_c=d_c),
        out_shape=jax.ShapeDtypeStruct((bsz, seq, d), f32),
        grid=(bsz, seq // MIX_ROWS),
        in_specs=in_specs,
        out_specs=row_spec,
        scratch_shapes=[
            pltpu.VMEM((MIX_ROWS, d_proj), f32),
            pltpu.VMEM((A_HALO + MIX_ROWS, d_a), f32),
            pltpu.VMEM((C_HALO + MIX_ROWS, d_c), f32),
            pltpu.VMEM((MIX_ROWS, d_a + d_b + d_c), bf16),
            pltpu.VMEM((heads_b * CHUNK, CHUNK), bf16),
        ],
        compiler_params=pltpu.CompilerParams(
            dimension_semantics=("arbitrary", "arbitrary"),
            vmem_limit_bytes=V7X_VMEM_LIMIT_BYTES),
        name="mixer",
    )(*args)


def kernel(x, ffn1_norm, ffn1_w_gate, ffn1_w_up, ffn1_w_down, mix_norm, w_in, conv_a_w, conv_a_b,
           ln_a_g, ln_a_b, ln_v_g, ln_v_b, w_s, b_s, conv_c_w, w_out, ffn2_norm, ffn2_w_gate,
           ffn2_w_up, ffn2_w_down, final_norm):
    bsz, seq, d = x.shape
    depth = w_in.shape[0]
    xf = x.reshape(bsz * seq, d)
    ffn1_w = [w.astype(bf16) for w in (ffn1_w_gate, ffn1_w_up, ffn1_w_down)]
    ffn2_w = [w.astype(bf16) for w in (ffn2_w_gate, ffn2_w_up, ffn2_w_down)]
    w_in_bf, w_out_bf = w_in.astype(bf16), w_out.astype(bf16)
    for l in range(depth):
        xf = _ffn(xf, ffn1_norm[l], *ffn1_w, layer=l)
        xm = _mixer(xf.reshape(bsz, seq, d), mix_norm[l], w_in_bf, conv_a_w[l], conv_a_b[l],
                    ln_a_g[l], ln_a_b[l], ln_v_g[l], ln_v_b[l], w_s[l], b_s[l], conv_c_w[l],
                    w_out_bf, layer=l)
        xf = _ffn(xm.reshape(bsz * seq, d), ffn2_norm[l], *ffn2_w, layer=l,
                  final_g=final_norm if l == depth - 1 else None)
    return xf.reshape(bsz, seq, d)
```

```python
import functools

import jax
import jax.numpy as jnp
from jax import lax
from jax.experimental import pallas as pl
from jax.experimental.pallas import tpu as pltpu

RMS_EPS = 1e-6
LN_EPS = 1e-5

HEAD_DIM = 64
CHUNK = 128
CONV_A_WIDTH = 31
CONV_C_WIDTH = 3

V7X_LANES = 128
V7X_MXU_COLS = 256
V7X_VMEM_LIMIT_BYTES = 56 * 1024 * 1024

FFN_ROWS = 1024
MIX_ROWS = 1024
A_HALO = 32
C_HALO = 8
CONV_ROW_BLOCK = 64
CONV_PHASES = 4
CONV_PHASE_ROWS = 64

f32 = jnp.float32
bf16 = jnp.bfloat16


def _rms_norm(x, g):
    return x * lax.rsqrt(jnp.mean(x * x, axis=-1, keepdims=True) + RMS_EPS) * g


def _layer_norm(x, g, b):
    mu = jnp.mean(x, axis=-1, keepdims=True)
    xc = x - mu
    var = jnp.mean(xc * xc, axis=-1, keepdims=True)
    return xc * lax.rsqrt(var + LN_EPS) * g + b


def _dot(a, b):
    return jnp.dot(a, b, preferred_element_type=f32)


def _ffn_kernel(*refs, ff_chunks, final):
    if final:
        x_ref, g_ref, wg_ref, wu_ref, wd_ref, fn_ref, o_ref, h_ref, acc_ref = refs
    else:
        x_ref, g_ref, wg_ref, wu_ref, wd_ref, o_ref, h_ref, acc_ref = refs
    h_ref[...] = _rms_norm(x_ref[...], g_ref[...]).astype(bf16)
    start = 0
    for width in ff_chunks:
        cols = slice(start, start + width)
        h = h_ref[...]
        gate = _dot(h, wg_ref[:, cols])
        up = _dot(h, wu_ref[:, cols])
        act = (gate * jax.nn.sigmoid(gate) * up).astype(bf16)
        down = _dot(act, wd_ref[cols, :])
        if start == 0:
            acc_ref[...] = down
        else:
            acc_ref[...] += down
        start += width
    y = x_ref[...] + 0.5 * acc_ref[...]
    if final:
        y = _rms_norm(y, fn_ref[...])
    o_ref[...] = y


def _ff_chunks(d_ff):
    assert d_ff % V7X_MXU_COLS == 0
    return (V7X_MXU_COLS,) * (d_ff // V7X_MXU_COLS)


def _resident(shape):
    return pl.BlockSpec(shape, lambda *_: (0,) * len(shape), pipeline_mode=pl.Buffered(1))


def _resident_layer(stacked, layer):
    tail = stacked.shape[1:]
    return pl.BlockSpec((None,) + tail, lambda *_: (layer,) + (0,) * len(tail),
                        pipeline_mode=pl.Buffered(1))


def _ffn(x, norm_g, w_gate, w_up, w_down, layer, final_g=None):
    n, d = x.shape
    d_ff = w_gate.shape[2]
    assert n % FFN_ROWS == 0
    final = final_g is not None
    row_spec = pl.BlockSpec((FFN_ROWS, d), lambda i: (i, 0))
    in_specs = [row_spec, _resident((1, d)), _resident_layer(w_gate, layer),
                _resident_layer(w_up, layer), _resident_layer(w_down, layer)]
    args = [x, norm_g.reshape(1, d), w_gate, w_up, w_down]
    if final:
        in_specs.append(_resident((1, d)))
        args.append(final_g.reshape(1, d))
    return pl.pallas_call(
        functools.partial(_ffn_kernel, ff_chunks=_ff_chunks(d_ff), final=final),
        out_shape=jax.ShapeDtypeStruct((n, d), f32),
        grid=(n // FFN_ROWS,),
        in_specs=in_specs,
        out_specs=row_spec,
        scratch_shapes=[pltpu.VMEM((FFN_ROWS, d), bf16), pltpu.VMEM((FFN_ROWS, d), f32)],
        compiler_params=pltpu.CompilerParams(
            dimension_semantics=("arbitrary",), vmem_limit_bytes=V7X_VMEM_LIMIT_BYTES),
        name="ffn_final" if final else "ffn",
    )(*args)


def _causal_conv(ext_ref, w_ref, out_ref, halo, width, rows, bias_ref):
    n = CONV_PHASE_ROWS
    span = n * CONV_PHASES
    first = halo - (width - 1)
    for c in range(ext_ref.shape[0]):
        lanes = slice(c * V7X_LANES, (c + 1) * V7X_LANES)
        for t0 in range(0, rows, span):
            for m in range(CONV_PHASES):
                acc = None if bias_ref is None else bias_ref[:, lanes]
                for k in range(width):
                    src = first + t0 + m + k
                    term = w_ref[k:k + 1, lanes] * ext_ref[c, pl.ds(src, n, stride=CONV_PHASES), :]
                    acc = term if acc is None else acc + term
                out_ref[c, pl.ds(t0 + m, n, stride=CONV_PHASES), :] = acc


def _lane_tiles(ref, r0, r1):
    return jnp.concatenate([ref[c, r0:r1, :] for c in range(ref.shape[0])], axis=1)


def _mixer_kernel(x_ref, g_ref, win_ref, caw_ref, cab_ref, lag_ref, lab_ref, lvg_ref, lvb_ref,
                  ws_ref, bsf_ref, ccw_ref, wout_ref, o_ref,
                  p_ref, aext_ref, cext_ref, aconv_ref, cconv_ref, mix_ref, wsm_ref,
                  *, d_a, d_b, d_c):
    rows = x_ref.shape[1]
    heads_b = d_b // HEAD_DIM

    @pl.when(pl.program_id(1) == 0)
    def _():
        aext_ref[:, 0:A_HALO, :] = jnp.zeros((aext_ref.shape[0], A_HALO, V7X_LANES), f32)
        cext_ref[:, 0:C_HALO, :] = jnp.zeros((cext_ref.shape[0], C_HALO, V7X_LANES), f32)
        t_idx = lax.broadcasted_iota(jnp.int32, (CHUNK, CHUNK), 0)
        s_idx = lax.broadcasted_iota(jnp.int32, (CHUNK, CHUNK), 1)
        for h in range(heads_b):
            wsm_ref[h * CHUNK:(h + 1) * CHUNK, :] = jnp.where(
                s_idx <= t_idx, ws_ref[h], 0.0).astype(bf16)

    h = _rms_norm(x_ref[0], g_ref[...]).astype(bf16)
    p_ref[...] = _dot(h, win_ref[...])

    o_av, o_ag, o_bu, o_bv = 0, d_a, 2 * d_a, 2 * d_a + d_b
    o_cb = 2 * d_a + 2 * d_b
    o_cc, o_cx = o_cb + d_c, o_cb + 2 * d_c

    for c in range(aext_ref.shape[0]):
        lo = c * V7X_LANES
        aext_ref[c, A_HALO:A_HALO + rows, :] = (
            p_ref[:, o_av + lo:o_av + lo + V7X_LANES]
            * jax.nn.sigmoid(p_ref[:, o_ag + lo:o_ag + lo + V7X_LANES]))
    _causal_conv(aext_ref, caw_ref, aconv_ref, A_HALO, CONV_A_WIDTH, rows, cab_ref)
    for r0 in range(0, rows, CONV_ROW_BLOCK):
        y = _layer_norm(_lane_tiles(aconv_ref, r0, r0 + CONV_ROW_BLOCK), lag_ref[...], lab_ref[...])
        mix_ref[r0:r0 + CONV_ROW_BLOCK, 0:d_a] = (y * jax.nn.sigmoid(y)).astype(bf16)
    aext_ref[:, 0:A_HALO, :] = aext_ref[:, rows:rows + A_HALO, :]

    lane_head = lax.broadcasted_iota(jnp.int32, (CHUNK, d_b), 1) // HEAD_DIM
    for c0 in range(0, rows, CHUNK):
        u = jax.nn.gelu(p_ref[c0:c0 + CHUNK, o_bu:o_bu + d_b])
        v = jax.nn.gelu(p_ref[c0:c0 + CHUNK, o_bv:o_bv + d_b])
        v = _layer_norm(v, lvg_ref[...], lvb_ref[...]).astype(bf16)
        r = _dot(wsm_ref[...], v)
        s = r[0:CHUNK]
        for hd in range(1, heads_b):
            s = jnp.where(lane_head == hd, r[hd * CHUNK:(hd + 1) * CHUNK], s)
        mix_ref[c0:c0 + CHUNK, d_a:d_a + d_b] = (u * (s + bsf_ref[...])).astype(bf16)

    for c in range(cext_ref.shape[0]):
        lo = c * V7X_LANES
        cext_ref[c, C_HALO:C_HALO + rows, :] = (
            p_ref[:, o_cc + lo:o_cc + lo + V7X_LANES] * p_ref[:, o_cx + lo:o_cx + lo + V7X_LANES])
    _causal_conv(cext_ref, ccw_ref, cconv_ref, C_HALO, CONV_C_WIDTH, rows, None)
    for r0 in range(0, rows, CONV_ROW_BLOCK):
        mix_ref[r0:r0 + CONV_ROW_BLOCK, d_a + d_b:d_a + d_b + d_c] = (
            p_ref[r0:r0 + CONV_ROW_BLOCK, o_cb:o_cb + d_c]
            * _lane_tiles(cconv_ref, r0, r0 + CONV_ROW_BLOCK)).astype(bf16)
    cext_ref[:, 0:C_HALO, :] = cext_ref[:, rows:rows + C_HALO, :]

    o_ref[0] = x_ref[0] + _dot(mix_ref[...], wout_ref[...])


def _mixer(x, norm_g, w_in, conv_a_w, conv_a_b, ln_a_g, ln_a_b, ln_v_g, ln_v_b, w_s, b_s,
           conv_c_w, w_out, layer):
    bsz, seq, d = x.shape
    d_a = conv_a_w.shape[1]
    d_b = ln_v_g.shape[0]
    d_c = conv_c_w.shape[1]
    heads_b = d_b // HEAD_DIM
    d_proj = w_in.shape[2]
    assert seq % MIX_ROWS == 0 and MIX_ROWS % CHUNK == 0 and MIX_ROWS % CONV_ROW_BLOCK == 0
    assert MIX_ROWS % (CONV_PHASES * CONV_PHASE_ROWS) == 0
    assert d_a % V7X_LANES == 0 and d_c % V7X_LANES == 0
    assert w_s.shape == (heads_b, CHUNK, CHUNK) and d_proj == 2 * d_a + 2 * d_b + 3 * d_c
    bs_full = jnp.repeat(b_s.T, HEAD_DIM, axis=1)
    row_spec = pl.BlockSpec((1, MIX_ROWS, d), lambda b, t: (b, t, 0))
    vec = lambda a: a.reshape(1, -1)
    args = [x, vec(norm_g), w_in, conv_a_w, vec(conv_a_b), vec(ln_a_g), vec(ln_a_b),
            vec(ln_v_g), vec(ln_v_b), w_s, bs_full, conv_c_w, w_out]
    in_specs = [row_spec] + [
        _resident_layer(a, layer) if a is w_in or a is w_out else _resident(a.shape)
        for a in args[1:]]
    return pl.pallas_call(
        functools.partial(_mixer_kernel, d_a=d_a, d_b=d_b, d_c=d_c),
        out_shape=jax.ShapeDtypeStruct((bsz, seq, d), f32),
        grid=(bsz, seq // MIX_ROWS),
        in_specs=in_specs,
        out_specs=row_spec,
        scratch_shapes=[
            pltpu.VMEM((MIX_ROWS, d_proj), f32),
            pltpu.VMEM((d_a // V7X_LANES, A_HALO + MIX_ROWS, V7X_LANES), f32),
            pltpu.VMEM((d_c // V7X_LANES, C_HALO + MIX_ROWS, V7X_LANES), f32),
            pltpu.VMEM((d_a // V7X_LANES, MIX_ROWS, V7X_LANES), f32),
            pltpu.VMEM((d_c // V7X_LANES, MIX_ROWS, V7X_LANES), f32),
            pltpu.VMEM((MIX_ROWS, d_a + d_b + d_c), bf16),
            pltpu.VMEM((heads_b * CHUNK, CHUNK), bf16),
        ],
        compiler_params=pltpu.CompilerParams(
            dimension_semantics=("arbitrary", "arbitrary"),
            vmem_limit_bytes=V7X_VMEM_LIMIT_BYTES),
        name="mixer",
    )(*args)


def kernel(x, ffn1_norm, ffn1_w_gate, ffn1_w_up, ffn1_w_down, mix_norm, w_in, conv_a_w, conv_a_b,
           ln_a_g, ln_a_b, ln_v_g, ln_v_b, w_s, b_s, conv_c_w, w_out, ffn2_norm, ffn2_w_gate,
           ffn2_w_up, ffn2_w_down, final_norm):
    bsz, seq, d = x.shape
    depth = w_in.shape[0]
    xf = x.reshape(bsz * seq, d)
    ffn1_w = [w.astype(bf16) for w in (ffn1_w_gate, ffn1_w_up, ffn1_w_down)]
    ffn2_w = [w.astype(bf16) for w in (ffn2_w_gate, ffn2_w_up, ffn2_w_down)]
    w_in_bf, w_out_bf = w_in.astype(bf16), w_out.astype(bf16)
    for l in range(depth):
        xf = _ffn(xf, ffn1_norm[l], *ffn1_w, layer=l)
        xm = _mixer(xf.reshape(bsz, seq, d), mix_norm[l], w_in_bf, conv_a_w[l], conv_a_b[l],
                    ln_a_g[l], ln_a_b[l], ln_v_g[l], ln_v_b[l], w_s[l], b_s[l], conv_c_w[l],
                    w_out_bf, layer=l)
        xf = _ffn(xm.reshape(bsz * seq, d), ffn2_norm[l], *ffn2_w, layer=l,
                  final_g=final_norm if l == depth - 1 else None)
    return xf.reshape(bsz, seq, d)
```

```python
import functools

import jax
import jax.numpy as jnp
from jax import lax
from jax.experimental import pallas as pl
from jax.experimental.pallas import tpu as pltpu

RMS_EPS = 1e-6
LN_EPS = 1e-5

HEAD_DIM = 64
CHUNK = 128
CONV_A_WIDTH = 31
CONV_C_WIDTH = 3

V7X_LANES = 128
V7X_BF16_SUBLANES = 16
V7X_MXU_COLS = 256
V7X_VMEM_LIMIT_BYTES = 56 * 1024 * 1024

FFN_ROWS = 1024
MIX_ROWS = 1024
A_HALO = 32
C_HALO = 8
CONV_ROW_BLOCK = 64
CONV_PHASES = 4
CONV_PHASE_ROWS = 64

f32 = jnp.float32
bf16 = jnp.bfloat16


def _rms_norm(x, g):
    return x * lax.rsqrt(jnp.mean(x * x, axis=-1, keepdims=True) + RMS_EPS) * g


def _layer_norm(x, g, b):
    mu = jnp.mean(x, axis=-1, keepdims=True)
    xc = x - mu
    var = jnp.mean(xc * xc, axis=-1, keepdims=True)
    return xc * lax.rsqrt(var + LN_EPS) * g + b


def _dot(a, b):
    return jnp.dot(a, b, preferred_element_type=f32)


def _cast_slabs(src_refs, dst_refs):
    for src, dst in zip(src_refs, dst_refs):
        dst[...] = src[...].astype(bf16)


def _ffn_kernel(*refs, ff_chunks, final, n_cast):
    x_ref, g_ref, wg_ref, wu_ref, wd_ref = refs[:5]
    rest = refs[5:]
    if final:
        fn_ref, rest = rest[0], rest[1:]
    cast_src, o_ref, cast_dst = rest[:n_cast], rest[n_cast], rest[n_cast + 1:2 * n_cast + 1]
    h_ref, acc_ref = rest[2 * n_cast + 1:]
    _cast_slabs(cast_src, cast_dst)
    h_ref[...] = _rms_norm(x_ref[...], g_ref[...]).astype(bf16)
    start = 0
    for width in ff_chunks:
        cols = slice(start, start + width)
        h = h_ref[...]
        gate = _dot(h, wg_ref[:, cols])
        up = _dot(h, wu_ref[:, cols])
        act = (gate * jax.nn.sigmoid(gate) * up).astype(bf16)
        down = _dot(act, wd_ref[cols, :])
        if start == 0:
            acc_ref[...] = down
        else:
            acc_ref[...] += down
        start += width
    y = x_ref[...] + 0.5 * acc_ref[...]
    if final:
        y = _rms_norm(y, fn_ref[...])
    o_ref[...] = y


def _ff_chunks(d_ff):
    assert d_ff % V7X_MXU_COLS == 0
    return (V7X_MXU_COLS,) * (d_ff // V7X_MXU_COLS)


def _resident(shape):
    return pl.BlockSpec(shape, lambda *_: (0,) * len(shape), pipeline_mode=pl.Buffered(1))


def _cast_specs(stacked, layer, n_steps, step_of):
    _, rows, cols = stacked.shape
    n_slabs = n_steps
    while rows % n_slabs or (rows // n_slabs) % V7X_BF16_SUBLANES:
        assert n_slabs % 2 == 0
        n_slabs //= 2
    slab = rows // n_slabs

    def slab_of(*g):
        return jnp.minimum(step_of(*g), n_slabs - 1)

    return (pl.BlockSpec((None, slab, cols), lambda *g: (layer, slab_of(*g), 0)),
            pl.BlockSpec((slab, cols), lambda *g: (slab_of(*g), 0)),
            jax.ShapeDtypeStruct((rows, cols), bf16))


def _ffn(x, norm_g, w_gate, w_up, w_down, final_g=None, cast_next=()):
    n, d = x.shape
    d_ff = w_gate.shape[1]
    assert n % FFN_ROWS == 0
    final = final_g is not None
    n_steps = n // FFN_ROWS
    row_spec = pl.BlockSpec((FFN_ROWS, d), lambda i: (i, 0))
    in_specs = [row_spec, _resident((1, d)), _resident(w_gate.shape), _resident(w_up.shape),
                _resident(w_down.shape)]
    args = [x, norm_g.reshape(1, d), w_gate, w_up, w_down]
    if final:
        in_specs.append(_resident((1, d)))
        args.append(final_g.reshape(1, d))
    casts = [_cast_specs(w, layer, n_steps, lambda i: i) for w, layer in cast_next]
    in_specs += [c[0] for c in casts]
    args += [w for w, _ in cast_next]
    y, *copies = pl.pallas_call(
        functools.partial(_ffn_kernel, ff_chunks=_ff_chunks(d_ff), final=final,
                          n_cast=len(casts)),
        out_shape=[jax.ShapeDtypeStruct((n, d), f32)] + [c[2] for c in casts],
        grid=(n_steps,),
        in_specs=in_specs,
        out_specs=[row_spec] + [c[1] for c in casts],
        scratch_shapes=[pltpu.VMEM((FFN_ROWS, d), bf16), pltpu.VMEM((FFN_ROWS, d), f32)],
        compiler_params=pltpu.CompilerParams(
            dimension_semantics=("arbitrary",), vmem_limit_bytes=V7X_VMEM_LIMIT_BYTES),
        name="ffn_final" if final else "ffn",
    )(*args)
    return y, copies


def _causal_conv(ext_ref, w_ref, out_ref, halo, width, rows, bias_ref):
    n = CONV_PHASE_ROWS
    span = n * CONV_PHASES
    first = halo - (width - 1)
    for c in range(ext_ref.shape[0]):
        lanes = slice(c * V7X_LANES, (c + 1) * V7X_LANES)
        for t0 in range(0, rows, span):
            for m in range(CONV_PHASES):
                acc = None if bias_ref is None else bias_ref[:, lanes]
                for k in range(width):
                    src = first + t0 + m + k
                    term = w_ref[k:k + 1, lanes] * ext_ref[c, pl.ds(src, n, stride=CONV_PHASES), :]
                    acc = term if acc is None else acc + term
                out_ref[c, pl.ds(t0 + m, n, stride=CONV_PHASES), :] = acc


def _lane_tiles(ref, r0, r1):
    return jnp.concatenate([ref[c, r0:r1, :] for c in range(ref.shape[0])], axis=1)


def _mixer_kernel(*refs, d_a, d_b, d_c, n_cast):
    (x_ref, g_ref, win_ref, caw_ref, cab_ref, lag_ref, lab_ref, lvg_ref, lvb_ref,
     ws_ref, bsf_ref, ccw_ref, wout_ref) = refs[:13]
    rest = refs[13:]
    cast_src, o_ref, cast_dst = rest[:n_cast], rest[n_cast], rest[n_cast + 1:2 * n_cast + 1]
    p_ref, aext_ref, cext_ref, aconv_ref, cconv_ref, mix_ref, wsm_ref = rest[2 * n_cast + 1:]
    _cast_slabs(cast_src, cast_dst)
    rows = x_ref.shape[1]
    heads_b = d_b // HEAD_DIM

    @pl.when(pl.program_id(1) == 0)
    def _():
        aext_ref[:, 0:A_HALO, :] = jnp.zeros((aext_ref.shape[0], A_HALO, V7X_LANES), f32)
        cext_ref[:, 0:C_HALO, :] = jnp.zeros((cext_ref.shape[0], C_HALO, V7X_LANES), f32)
        t_idx = lax.broadcasted_iota(jnp.int32, (CHUNK, CHUNK), 0)
        s_idx = lax.broadcasted_iota(jnp.int32, (CHUNK, CHUNK), 1)
        for h in range(heads_b):
            wsm_ref[h * CHUNK:(h + 1) * CHUNK, :] = jnp.where(
                s_idx <= t_idx, ws_ref[h], 0.0).astype(bf16)

    h = _rms_norm(x_ref[0], g_ref[...]).astype(bf16)
    p_ref[...] = _dot(h, win_ref[...])

    o_av, o_ag, o_bu, o_bv = 0, d_a, 2 * d_a, 2 * d_a + d_b
    o_cb = 2 * d_a + 2 * d_b
    o_cc, o_cx = o_cb + d_c, o_cb + 2 * d_c

    for c in range(aext_ref.shape[0]):
        lo = c * V7X_LANES
        aext_ref[c, A_HALO:A_HALO + rows, :] = (
            p_ref[:, o_av + lo:o_av + lo + V7X_LANES]
            * jax.nn.sigmoid(p_ref[:, o_ag + lo:o_ag + lo + V7X_LANES]))
    _causal_conv(aext_ref, caw_ref, aconv_ref, A_HALO, CONV_A_WIDTH, rows, cab_ref)
    for r0 in range(0, rows, CONV_ROW_BLOCK):
        y = _layer_norm(_lane_tiles(aconv_ref, r0, r0 + CONV_ROW_BLOCK), lag_ref[...], lab_ref[...])
        mix_ref[r0:r0 + CONV_ROW_BLOCK, 0:d_a] = (y * jax.nn.sigmoid(y)).astype(bf16)
    aext_ref[:, 0:A_HALO, :] = aext_ref[:, rows:rows + A_HALO, :]

    lane_head = lax.broadcasted_iota(jnp.int32, (CHUNK, d_b), 1) // HEAD_DIM
    for c0 in range(0, rows, CHUNK):
        u = jax.nn.gelu(p_ref[c0:c0 + CHUNK, o_bu:o_bu + d_b])
        v = jax.nn.gelu(p_ref[c0:c0 + CHUNK, o_bv:o_bv + d_b])
        v = _layer_norm(v, lvg_ref[...], lvb_ref[...]).astype(bf16)
        r = _dot(wsm_ref[...], v)
        s = r[0:CHUNK]
        for hd in range(1, heads_b):
            s = jnp.where(lane_head == hd, r[hd * CHUNK:(hd + 1) * CHUNK], s)
        mix_ref[c0:c0 + CHUNK, d_a:d_a + d_b] = (u * (s + bsf_ref[...])).astype(bf16)

    for c in range(cext_ref.shape[0]):
        lo = c * V7X_LANES
        cext_ref[c, C_HALO:C_HALO + rows, :] = (
            p_ref[:, o_cc + lo:o_cc + lo + V7X_LANES] * p_ref[:, o_cx + lo:o_cx + lo + V7X_LANES])
    _causal_conv(cext_ref, ccw_ref, cconv_ref, C_HALO, CONV_C_WIDTH, rows, None)
    for r0 in range(0, rows, CONV_ROW_BLOCK):
        mix_ref[r0:r0 + CONV_ROW_BLOCK, d_a + d_b:d_a + d_b + d_c] = (
            p_ref[r0:r0 + CONV_ROW_BLOCK, o_cb:o_cb + d_c]
            * _lane_tiles(cconv_ref, r0, r0 + CONV_ROW_BLOCK)).astype(bf16)
    cext_ref[:, 0:C_HALO, :] = cext_ref[:, rows:rows + C_HALO, :]

    o_ref[0] = x_ref[0] + _dot(mix_ref[...], wout_ref[...])


def _mixer(x, norm_g, w_in, conv_a_w, conv_a_b, ln_a_g, ln_a_b, ln_v_g, ln_v_b, w_s, b_s,
           conv_c_w, w_out, cast_next=()):
    bsz, seq, d = x.shape
    d_a = conv_a_w.shape[1]
    d_b = ln_v_g.shape[0]
    d_c = conv_c_w.shape[1]
    heads_b = d_b // HEAD_DIM
    d_proj = w_in.shape[1]
    assert seq % MIX_ROWS == 0 and MIX_ROWS % CHUNK == 0 and MIX_ROWS % CONV_ROW_BLOCK == 0
    assert MIX_ROWS % (CONV_PHASES * CONV_PHASE_ROWS) == 0
    assert d_a % V7X_LANES == 0 and d_c % V7X_LANES == 0
    assert w_s.shape == (heads_b, CHUNK, CHUNK) and d_proj == 2 * d_a + 2 * d_b + 3 * d_c
    bs_full = jnp.repeat(b_s.T, HEAD_DIM, axis=1)
    row_spec = pl.BlockSpec((1, MIX_ROWS, d), lambda b, t: (b, t, 0))
    vec = lambda a: a.reshape(1, -1)
    args = [x, vec(norm_g), w_in, conv_a_w, vec(conv_a_b), vec(ln_a_g), vec(ln_a_b),
            vec(ln_v_g), vec(ln_v_b), w_s, bs_full, conv_c_w, w_out]
    in_specs = [row_spec] + [_resident(a.shape) for a in args[1:]]
    tiles = seq // MIX_ROWS
    casts = [_cast_specs(w, layer, bsz * tiles, lambda b, t: b * tiles + t)
             for w, layer in cast_next]
    in_specs += [c[0] for c in casts]
    args += [w for w, _ in cast_next]
    y, *copies = pl.pallas_call(
        functools.partial(_mixer_kernel, d_a=d_a, d_b=d_b, d_c=d_c, n_cast=len(casts)),
        out_shape=[jax.ShapeDtypeStruct((bsz, seq, d), f32)] + [c[2] for c in casts],
        grid=(bsz, tiles),
        in_specs=in_specs,
        out_specs=[row_spec] + [c[1] for c in casts],
        scratch_shapes=[
            pltpu.VMEM((MIX_ROWS, d_proj), f32),
            pltpu.VMEM((d_a // V7X_LANES, A_HALO + MIX_ROWS, V7X_LANES), f32),
            pltpu.VMEM((d_c // V7X_LANES, C_HALO + MIX_ROWS, V7X_LANES), f32),
            pltpu.VMEM((d_a // V7X_LANES, MIX_ROWS, V7X_LANES), f32),
            pltpu.VMEM((d_c // V7X_LANES, MIX_ROWS, V7X_LANES), f32),
            pltpu.VMEM((MIX_ROWS, d_a + d_b + d_c), bf16),
            pltpu.VMEM((heads_b * CHUNK, CHUNK), bf16),
        ],
        compiler_params=pltpu.CompilerParams(
            dimension_semantics=("arbitrary", "arbitrary"),
            vmem_limit_bytes=V7X_VMEM_LIMIT_BYTES),
        name="mixer",
    )(*args)
    return y, copies


def kernel(x, ffn1_norm, ffn1_w_gate, ffn1_w_up, ffn1_w_down, mix_norm, w_in, conv_a_w, conv_a_b,
           ln_a_g, ln_a_b, ln_v_g, ln_v_b, w_s, b_s, conv_c_w, w_out, ffn2_norm, ffn2_w_gate,
           ffn2_w_up, ffn2_w_down, final_norm):
    bsz, seq, d = x.shape
    depth = w_in.shape[0]
    xf = x.reshape(bsz * seq, d)
    ffn1_w = (ffn1_w_gate, ffn1_w_up, ffn1_w_down)
    ffn2_w = (ffn2_w_gate, ffn2_w_up, ffn2_w_down)
    w1 = [w[0].astype(bf16) for w in ffn1_w]
    for l in range(depth):
        xf, (w_in_bf, w_out_bf) = _ffn(xf, ffn1_norm[l], *w1, cast_next=((w_in, l), (w_out, l)))
        xm, w2 = _mixer(xf.reshape(bsz, seq, d), mix_norm[l], w_in_bf, conv_a_w[l], conv_a_b[l],
                        ln_a_g[l], ln_a_b[l], ln_v_g[l], ln_v_b[l], w_s[l], b_s[l], conv_c_w[l],
                        w_out_bf, cast_next=[(w, l) for w in ffn2_w])
        last = l == depth - 1
        xf, w1 = _ffn(xm.reshape(bsz * seq, d), ffn2_norm[l], *w2,
                      final_g=final_norm if last else None,
                      cast_next=() if last else [(w, l + 1) for w in ffn1_w])
    return xf.reshape(bsz, seq, d)
```

```python
import functools

import jax
import jax.numpy as jnp
from jax import lax
from jax.experimental import pallas as pl
from jax.experimental.pallas import tpu as pltpu

RMS_EPS = 1e-6
LN_EPS = 1e-5

HEAD_DIM = 64
CHUNK = 128
CONV_A_WIDTH = 31
CONV_C_WIDTH = 3

V7X_LANES = 128
V7X_BF16_SUBLANES = 16
V7X_MXU_COLS = 256
V7X_VMEM_LIMIT_BYTES = 56 * 1024 * 1024

FFN_ROWS = 1024
MIX_ROWS = 1024
A_HALO = 32
C_HALO = 8
CONV_ROW_BLOCK = 64
CONV_PHASES = 2
CONV_PHASE_ROWS = 64

f32 = jnp.float32
bf16 = jnp.bfloat16


def _rms_norm(x, g):
    return x * lax.rsqrt(jnp.mean(x * x, axis=-1, keepdims=True) + RMS_EPS) * g


def _layer_norm(x, g, b):
    mu = jnp.mean(x, axis=-1, keepdims=True)
    xc = x - mu
    var = jnp.mean(xc * xc, axis=-1, keepdims=True)
    return xc * lax.rsqrt(var + LN_EPS) * g + b


def _dot(a, b):
    return jnp.dot(a, b, preferred_element_type=f32)


def _cast_slabs(src_refs, dst_refs):
    for src, dst in zip(src_refs, dst_refs):
        dst[...] = src[...].astype(bf16)


def _ffn_kernel(*refs, ff_chunks, final, n_cast):
    x_ref, g_ref, wg_ref, wu_ref, wd_ref = refs[:5]
    rest = refs[5:]
    if final:
        fn_ref, rest = rest[0], rest[1:]
    cast_src, o_ref, cast_dst = rest[:n_cast], rest[n_cast], rest[n_cast + 1:2 * n_cast + 1]
    h_ref, acc_ref = rest[2 * n_cast + 1:]
    _cast_slabs(cast_src, cast_dst)
    h_ref[...] = _rms_norm(x_ref[...], g_ref[...]).astype(bf16)
    start = 0
    for width in ff_chunks:
        cols = slice(start, start + width)
        h = h_ref[...]
        gate = _dot(h, wg_ref[:, cols])
        up = _dot(h, wu_ref[:, cols])
        act = (gate * jax.nn.sigmoid(gate) * up).astype(bf16)
        down = _dot(act, wd_ref[cols, :])
        if start == 0:
            acc_ref[...] = down
        else:
            acc_ref[...] += down
        start += width
    y = x_ref[...] + 0.5 * acc_ref[...]
    if final:
        y = _rms_norm(y, fn_ref[...])
    o_ref[...] = y


def _ff_chunks(d_ff):
    assert d_ff % V7X_MXU_COLS == 0
    return (V7X_MXU_COLS,) * (d_ff // V7X_MXU_COLS)


def _resident(shape):
    return pl.BlockSpec(shape, lambda *_: (0,) * len(shape), pipeline_mode=pl.Buffered(1))


def _cast_specs(stacked, layer, n_steps, step_of):
    _, rows, cols = stacked.shape
    n_slabs = n_steps
    while rows % n_slabs or (rows // n_slabs) % V7X_BF16_SUBLANES:
        assert n_slabs % 2 == 0
        n_slabs //= 2
    slab = rows // n_slabs

    def slab_of(*g):
        return jnp.minimum(step_of(*g), n_slabs - 1)

    return (pl.BlockSpec((None, slab, cols), lambda *g: (layer, slab_of(*g), 0)),
            pl.BlockSpec((slab, cols), lambda *g: (slab_of(*g), 0)),
            jax.ShapeDtypeStruct((rows, cols), bf16))


def _ffn(x, norm_g, w_gate, w_up, w_down, final_g=None, cast_next=()):
    n, d = x.shape
    d_ff = w_gate.shape[1]
    assert n % FFN_ROWS == 0
    final = final_g is not None
    n_steps = n // FFN_ROWS
    row_spec = pl.BlockSpec((FFN_ROWS, d), lambda i: (i, 0))
    in_specs = [row_spec, _resident((1, d)), _resident(w_gate.shape), _resident(w_up.shape),
                _resident(w_down.shape)]
    args = [x, norm_g.reshape(1, d), w_gate, w_up, w_down]
    if final:
        in_specs.append(_resident((1, d)))
        args.append(final_g.reshape(1, d))
    casts = [_cast_specs(w, layer, n_steps, lambda i: i) for w, layer in cast_next]
    in_specs += [c[0] for c in casts]
    args += [w for w, _ in cast_next]
    y, *copies = pl.pallas_call(
        functools.partial(_ffn_kernel, ff_chunks=_ff_chunks(d_ff), final=final,
                          n_cast=len(casts)),
        out_shape=[jax.ShapeDtypeStruct((n, d), f32)] + [c[2] for c in casts],
        grid=(n_steps,),
        in_specs=in_specs,
        out_specs=[row_spec] + [c[1] for c in casts],
        scratch_shapes=[pltpu.VMEM((FFN_ROWS, d), bf16), pltpu.VMEM((FFN_ROWS, d), f32)],
        compiler_params=pltpu.CompilerParams(
            dimension_semantics=("arbitrary",), vmem_limit_bytes=V7X_VMEM_LIMIT_BYTES),
        name="ffn_final" if final else "ffn",
    )(*args)
    return y, copies


def _causal_conv(ext_ref, w_ref, out_ref, halo, width, rows, bias_ref):
    n = CONV_PHASE_ROWS
    span = n * CONV_PHASES
    first = halo - (width - 1)
    for c in range(ext_ref.shape[0]):
        lanes = slice(c * V7X_LANES, (c + 1) * V7X_LANES)
        for t0 in range(0, rows, span):
            for m in range(CONV_PHASES):
                acc = None if bias_ref is None else bias_ref[:, lanes]
                for k in range(width):
                    src = first + t0 + m + k
                    term = w_ref[k:k + 1, lanes] * ext_ref[c, pl.ds(src, n, stride=CONV_PHASES), :]
                    acc = term if acc is None else acc + term
                out_ref[c, pl.ds(t0 + m, n, stride=CONV_PHASES), :] = acc


def _lane_tiles(ref, r0, r1):
    return jnp.concatenate([ref[c, r0:r1, :] for c in range(ref.shape[0])], axis=1)


def _mixer_kernel(*refs, d_a, d_b, d_c, n_cast):
    (x_ref, g_ref, win_ref, caw_ref, cab_ref, lag_ref, lab_ref, lvg_ref, lvb_ref,
     ws_ref, bsf_ref, ccw_ref, wout_ref) = refs[:13]
    rest = refs[13:]
    cast_src, o_ref, cast_dst = rest[:n_cast], rest[n_cast], rest[n_cast + 1:2 * n_cast + 1]
    p_ref, aext_ref, cext_ref, aconv_ref, cconv_ref, mix_ref, wsm_ref = rest[2 * n_cast + 1:]
    _cast_slabs(cast_src, cast_dst)
    rows = x_ref.shape[1]
    heads_b = d_b // HEAD_DIM

    @pl.when(pl.program_id(1) == 0)
    def _():
        aext_ref[:, 0:A_HALO, :] = jnp.zeros((aext_ref.shape[0], A_HALO, V7X_LANES), f32)
        cext_ref[:, 0:C_HALO, :] = jnp.zeros((cext_ref.shape[0], C_HALO, V7X_LANES), f32)
        t_idx = lax.broadcasted_iota(jnp.int32, (CHUNK, CHUNK), 0)
        s_idx = lax.broadcasted_iota(jnp.int32, (CHUNK, CHUNK), 1)
        for h in range(heads_b):
            wsm_ref[h * CHUNK:(h + 1) * CHUNK, :] = jnp.where(
                s_idx <= t_idx, ws_ref[h], 0.0).astype(bf16)

    h = _rms_norm(x_ref[0], g_ref[...]).astype(bf16)
    p_ref[...] = _dot(h, win_ref[...])

    o_av, o_ag, o_bu, o_bv = 0, d_a, 2 * d_a, 2 * d_a + d_b
    o_cb = 2 * d_a + 2 * d_b
    o_cc, o_cx = o_cb + d_c, o_cb + 2 * d_c

    for c in range(aext_ref.shape[0]):
        lo = c * V7X_LANES
        aext_ref[c, A_HALO:A_HALO + rows, :] = (
            p_ref[:, o_av + lo:o_av + lo + V7X_LANES]
            * jax.nn.sigmoid(p_ref[:, o_ag + lo:o_ag + lo + V7X_LANES]))
    _causal_conv(aext_ref, caw_ref, aconv_ref, A_HALO, CONV_A_WIDTH, rows, cab_ref)
    for r0 in range(0, rows, CONV_ROW_BLOCK):
        y = _layer_norm(_lane_tiles(aconv_ref, r0, r0 + CONV_ROW_BLOCK), lag_ref[...], lab_ref[...])
        mix_ref[r0:r0 + CONV_ROW_BLOCK, 0:d_a] = (y * jax.nn.sigmoid(y)).astype(bf16)
    aext_ref[:, 0:A_HALO, :] = aext_ref[:, rows:rows + A_HALO, :]

    lane_head = lax.broadcasted_iota(jnp.int32, (CHUNK, d_b), 1) // HEAD_DIM
    for c0 in range(0, rows, CHUNK):
        u = jax.nn.gelu(p_ref[c0:c0 + CHUNK, o_bu:o_bu + d_b])
        v = jax.nn.gelu(p_ref[c0:c0 + CHUNK, o_bv:o_bv + d_b])
        v = _layer_norm(v, lvg_ref[...], lvb_ref[...]).astype(bf16)
        r = _dot(wsm_ref[...], v)
        s = r[0:CHUNK]
        for hd in range(1, heads_b):
            s = jnp.where(lane_head == hd, r[hd * CHUNK:(hd + 1) * CHUNK], s)
        mix_ref[c0:c0 + CHUNK, d_a:d_a + d_b] = (u * (s + bsf_ref[...])).astype(bf16)

    for c in range(cext_ref.shape[0]):
        lo = c * V7X_LANES
        cext_ref[c, C_HALO:C_HALO + rows, :] = (
            p_ref[:, o_cc + lo:o_cc + lo + V7X_LANES] * p_ref[:, o_cx + lo:o_cx + lo + V7X_LANES])
    _causal_conv(cext_ref, ccw_ref, cconv_ref, C_HALO, CONV_C_WIDTH, rows, None)
    for r0 in range(0, rows, CONV_ROW_BLOCK):
        mix_ref[r0:r0 + CONV_ROW_BLOCK, d_a + d_b:d_a + d_b + d_c] = (
            p_ref[r0:r0 + CONV_ROW_BLOCK, o_cb:o_cb + d_c]
            * _lane_tiles(cconv_ref, r0, r0 + CONV_ROW_BLOCK)).astype(bf16)
    cext_ref[:, 0:C_HALO, :] = cext_ref[:, rows:rows + C_HALO, :]

    o_ref[0] = x_ref[0] + _dot(mix_ref[...], wout_ref[...])


def _mixer(x, norm_g, w_in, conv_a_w, conv_a_b, ln_a_g, ln_a_b, ln_v_g, ln_v_b, w_s, b_s,
           conv_c_w, w_out, cast_next=()):
    bsz, seq, d = x.shape
    d_a = conv_a_w.shape[1]
    d_b = ln_v_g.shape[0]
    d_c = conv_c_w.shape[1]
    heads_b = d_b // HEAD_DIM
    d_proj = w_in.shape[1]
    assert seq % MIX_ROWS == 0 and MIX_ROWS % CHUNK == 0 and MIX_ROWS % CONV_ROW_BLOCK == 0
    assert MIX_ROWS % (CONV_PHASES * CONV_PHASE_ROWS) == 0
    assert d_a % V7X_LANES == 0 and d_c % V7X_LANES == 0
    assert w_s.shape == (heads_b, CHUNK, CHUNK) and d_proj == 2 * d_a + 2 * d_b + 3 * d_c
    bs_full = jnp.repeat(b_s.T, HEAD_DIM, axis=1)
    row_spec = pl.BlockSpec((1, MIX_ROWS, d), lambda b, t: (b, t, 0))
    vec = lambda a: a.reshape(1, -1)
    args = [x, vec(norm_g), w_in, conv_a_w, vec(conv_a_b), vec(ln_a_g), vec(ln_a_b),
            vec(ln_v_g), vec(ln_v_b), w_s, bs_full, conv_c_w, w_out]
    in_specs = [row_spec] + [_resident(a.shape) for a in args[1:]]
    tiles = seq // MIX_ROWS
    casts = [_cast_specs(w, layer, bsz * tiles, lambda b, t: b * tiles + t)
             for w, layer in cast_next]
    in_specs += [c[0] for c in casts]
    args += [w for w, _ in cast_next]
    y, *copies = pl.pallas_call(
        functools.partial(_mixer_kernel, d_a=d_a, d_b=d_b, d_c=d_c, n_cast=len(casts)),
        out_shape=[jax.ShapeDtypeStruct((bsz, seq, d), f32)] + [c[2] for c in casts],
        grid=(bsz, tiles),
        in_specs=in_specs,
        out_specs=[row_spec] + [c[1] for c in casts],
        scratch_shapes=[
            pltpu.VMEM((MIX_ROWS, d_proj), f32),
            pltpu.VMEM((d_a // V7X_LANES, A_HALO + MIX_ROWS, V7X_LANES), f32),
            pltpu.VMEM((d_c // V7X_LANES, C_HALO + MIX_ROWS, V7X_LANES), f32),
            pltpu.VMEM((d_a // V7X_LANES, MIX_ROWS, V7X_LANES), f32),
            pltpu.VMEM((d_c // V7X_LANES, MIX_ROWS, V7X_LANES), f32),
            pltpu.VMEM((MIX_ROWS, d_a + d_b + d_c), bf16),
            pltpu.VMEM((heads_b * CHUNK, CHUNK), bf16),
        ],
        compiler_params=pltpu.CompilerParams(
            dimension_semantics=("arbitrary", "arbitrary"),
            vmem_limit_bytes=V7X_VMEM_LIMIT_BYTES),
        name="mixer",
    )(*args)
    return y, copies


def kernel(x, ffn1_norm, ffn1_w_gate, ffn1_w_up, ffn1_w_down, mix_norm, w_in, conv_a_w, conv_a_b,
           ln_a_g, ln_a_b, ln_v_g, ln_v_b, w_s, b_s, conv_c_w, w_out, ffn2_norm, ffn2_w_gate,
           ffn2_w_up, ffn2_w_down, final_norm):
    bsz, seq, d = x.shape
    depth = w_in.shape[0]
    xf = x.reshape(bsz * seq, d)
    ffn1_w = (ffn1_w_gate, ffn1_w_up, ffn1_w_down)
    ffn2_w = (ffn2_w_gate, ffn2_w_up, ffn2_w_down)
    w1 = [w[0].astype(bf16) for w in ffn1_w]
    for l in range(depth):
        xf, (w_in_bf, w_out_bf) = _ffn(xf, ffn1_norm[l], *w1, cast_next=((w_in, l), (w_out, l)))
        xm, w2 = _mixer(xf.reshape(bsz, seq, d), mix_norm[l], w_in_bf, conv_a_w[l], conv_a_b[l],
                        ln_a_g[l], ln_a_b[l], ln_v_g[l], ln_v_b[l], w_s[l], b_s[l], conv_c_w[l],
                        w_out_bf, cast_next=[(w, l) for w in ffn2_w])
        last = l == depth - 1
        xf, w1 = _ffn(xm.reshape(bsz * seq, d), ffn2_norm[l], *w2,
                      final_g=final_norm if last else None,
                      cast_next=() if last else [(w, l + 1) for w in ffn1_w])
    return xf.reshape(bsz, seq, d)
```

```python
import functools

import jax
import jax.numpy as jnp
from jax import lax
from jax.experimental import pallas as pl
from jax.experimental.pallas import tpu as pltpu

RMS_EPS = 1e-6
LN_EPS = 1e-5

HEAD_DIM = 64
CHUNK = 128
CONV_A_WIDTH = 31
CONV_C_WIDTH = 3

V7X_LANES = 128
V7X_BF16_SUBLANES = 16
V7X_MXU_COLS = 256
V7X_VMEM_LIMIT_BYTES = 56 * 1024 * 1024

FFN_ROWS = 1024
MIX_ROWS = 1024
A_HALO = 32
C_HALO = 8
CONV_ROW_BLOCK = 64
CONV_PHASES = 2
CONV_PHASE_ROWS = 64

f32 = jnp.float32
bf16 = jnp.bfloat16


def _rms_norm(x, g):
    return x * lax.rsqrt(jnp.mean(x * x, axis=-1, keepdims=True) + RMS_EPS) * g


def _layer_norm(x, g, b):
    mu = jnp.mean(x, axis=-1, keepdims=True)
    xc = x - mu
    var = jnp.mean(xc * xc, axis=-1, keepdims=True)
    return xc * lax.rsqrt(var + LN_EPS) * g + b


def _dot(a, b):
    return jnp.dot(a, b, preferred_element_type=f32)


def _cast_slabs(src_refs, dst_refs):
    for src, dst in zip(src_refs, dst_refs):
        dst[...] = src[...].astype(bf16)


def _ffn_kernel(*refs, ff_chunks, final, n_cast):
    x_ref, g_ref, wg_ref, wu_ref, wd_ref = refs[:5]
    rest = refs[5:]
    if final:
        fn_ref, rest = rest[0], rest[1:]
    cast_src, o_ref, cast_dst = rest[:n_cast], rest[n_cast], rest[n_cast + 1:2 * n_cast + 1]
    h_ref, acc_ref = rest[2 * n_cast + 1:]
    _cast_slabs(cast_src, cast_dst)
    h_ref[...] = _rms_norm(x_ref[...], g_ref[...]).astype(bf16)
    start = 0
    for width in ff_chunks:
        cols = slice(start, start + width)
        h = h_ref[...]
        gate = _dot(h, wg_ref[:, cols])
        up = _dot(h, wu_ref[:, cols])
        act = (gate * jax.nn.sigmoid(gate) * up).astype(bf16)
        down = _dot(act, wd_ref[cols, :])
        if start == 0:
            acc_ref[...] = down
        else:
            acc_ref[...] += down
        start += width
    y = x_ref[...] + 0.5 * acc_ref[...]
    if final:
        y = _rms_norm(y, fn_ref[...])
    o_ref[...] = y


def _ff_chunks(d_ff):
    assert d_ff % V7X_MXU_COLS == 0
    return (V7X_MXU_COLS,) * (d_ff // V7X_MXU_COLS)


def _resident(shape):
    return pl.BlockSpec(shape, lambda *_: (0,) * len(shape), pipeline_mode=pl.Buffered(1))


def _cast_specs(stacked, layer, n_steps, step_of):
    _, rows, cols = stacked.shape
    n_slabs = n_steps
    while rows % n_slabs or (rows // n_slabs) % V7X_BF16_SUBLANES:
        assert n_slabs % 2 == 0
        n_slabs //= 2
    slab = rows // n_slabs

    def slab_of(*g):
        return jnp.minimum(step_of(*g), n_slabs - 1)

    return (pl.BlockSpec((None, slab, cols), lambda *g: (layer, slab_of(*g), 0)),
            pl.BlockSpec((slab, cols), lambda *g: (slab_of(*g), 0)),
            jax.ShapeDtypeStruct((rows, cols), bf16))


def _ffn(x, norm_g, w_gate, w_up, w_down, final_g=None, cast_next=()):
    n, d = x.shape
    d_ff = w_gate.shape[1]
    assert n % FFN_ROWS == 0
    final = final_g is not None
    n_steps = n // FFN_ROWS
    row_spec = pl.BlockSpec((FFN_ROWS, d), lambda i: (i, 0))
    in_specs = [row_spec, _resident((1, d)), _resident(w_gate.shape), _resident(w_up.shape),
                _resident(w_down.shape)]
    args = [x, norm_g.reshape(1, d), w_gate, w_up, w_down]
    if final:
        in_specs.append(_resident((1, d)))
        args.append(final_g.reshape(1, d))
    casts = [_cast_specs(w, layer, n_steps, lambda i: i) for w, layer in cast_next]
    in_specs += [c[0] for c in casts]
    args += [w for w, _ in cast_next]
    y, *copies = pl.pallas_call(
        functools.partial(_ffn_kernel, ff_chunks=_ff_chunks(d_ff), final=final,
                          n_cast=len(casts)),
        out_shape=[jax.ShapeDtypeStruct((n, d), f32)] + [c[2] for c in casts],
        grid=(n_steps,),
        in_specs=in_specs,
        out_specs=[row_spec] + [c[1] for c in casts],
        scratch_shapes=[pltpu.VMEM((FFN_ROWS, d), bf16), pltpu.VMEM((FFN_ROWS, d), f32)],
        compiler_params=pltpu.CompilerParams(
            dimension_semantics=("arbitrary",), vmem_limit_bytes=V7X_VMEM_LIMIT_BYTES),
        name="ffn_final" if final else "ffn",
    )(*args)
    return y, copies


def _causal_conv(ext_ref, w_ref, out_ref, halo, width, rows, bias_ref):
    n = CONV_PHASE_ROWS
    span = n * CONV_PHASES
    first = halo - (width - 1)
    for c in range(ext_ref.shape[0]):
        lanes = slice(c * V7X_LANES, (c + 1) * V7X_LANES)
        for t0 in range(0, rows, span):
            for m in range(CONV_PHASES):
                acc = None if bias_ref is None else bias_ref[:, lanes]
                for k in range(width):
                    src = first + t0 + m + k
                    term = w_ref[k:k + 1, lanes] * ext_ref[c, pl.ds(src, n, stride=CONV_PHASES), :]
                    acc = term if acc is None else acc + term
                out_ref[c, pl.ds(t0 + m, n, stride=CONV_PHASES), :] = acc


def _lane_tiles(ref, r0, r1):
    return jnp.concatenate([ref[c, r0:r1, :] for c in range(ref.shape[0])], axis=1)


def _mixer_kernel(*refs, d_a, d_b, d_c, n_cast, tiles_per_seq):
    (x_ref, g_ref, win_ref, caw_ref, cab_ref, lag_ref, lab_ref, lvg_ref, lvb_ref,
     ws_ref, bsf_ref, ccw_ref, wout_ref) = refs[:13]
    rest = refs[13:]
    cast_src, o_ref, cast_dst = rest[:n_cast], rest[n_cast], rest[n_cast + 1:2 * n_cast + 1]
    p_ref, aext_ref, cext_ref, aconv_ref, cconv_ref, mix_ref, wsm_ref = rest[2 * n_cast + 1:]
    _cast_slabs(cast_src, cast_dst)
    rows = x_ref.shape[0]
    heads_b = d_b // HEAD_DIM

    @pl.when(lax.rem(pl.program_id(0), tiles_per_seq) == 0)
    def _():
        aext_ref[:, 0:A_HALO, :] = jnp.zeros((aext_ref.shape[0], A_HALO, V7X_LANES), f32)
        cext_ref[:, 0:C_HALO, :] = jnp.zeros((cext_ref.shape[0], C_HALO, V7X_LANES), f32)
        t_idx = lax.broadcasted_iota(jnp.int32, (CHUNK, CHUNK), 0)
        s_idx = lax.broadcasted_iota(jnp.int32, (CHUNK, CHUNK), 1)
        for h in range(heads_b):
            wsm_ref[h * CHUNK:(h + 1) * CHUNK, :] = jnp.where(
                s_idx <= t_idx, ws_ref[h], 0.0).astype(bf16)

    h = _rms_norm(x_ref[...], g_ref[...]).astype(bf16)
    p_ref[...] = _dot(h, win_ref[...])

    o_av, o_ag, o_bu, o_bv = 0, d_a, 2 * d_a, 2 * d_a + d_b
    o_cb = 2 * d_a + 2 * d_b
    o_cc, o_cx = o_cb + d_c, o_cb + 2 * d_c

    for c in range(aext_ref.shape[0]):
        lo = c * V7X_LANES
        aext_ref[c, A_HALO:A_HALO + rows, :] = (
            p_ref[:, o_av + lo:o_av + lo + V7X_LANES]
            * jax.nn.sigmoid(p_ref[:, o_ag + lo:o_ag + lo + V7X_LANES]))
    _causal_conv(aext_ref, caw_ref, aconv_ref, A_HALO, CONV_A_WIDTH, rows, cab_ref)
    for r0 in range(0, rows, CONV_ROW_BLOCK):
        y = _layer_norm(_lane_tiles(aconv_ref, r0, r0 + CONV_ROW_BLOCK), lag_ref[...], lab_ref[...])
        mix_ref[r0:r0 + CONV_ROW_BLOCK, 0:d_a] = (y * jax.nn.sigmoid(y)).astype(bf16)
    aext_ref[:, 0:A_HALO, :] = aext_ref[:, rows:rows + A_HALO, :]

    lane_head = lax.broadcasted_iota(jnp.int32, (CHUNK, d_b), 1) // HEAD_DIM
    for c0 in range(0, rows, CHUNK):
        u = jax.nn.gelu(p_ref[c0:c0 + CHUNK, o_bu:o_bu + d_b])
        v = jax.nn.gelu(p_ref[c0:c0 + CHUNK, o_bv:o_bv + d_b])
        v = _layer_norm(v, lvg_ref[...], lvb_ref[...]).astype(bf16)
        r = _dot(wsm_ref[...], v)
        s = r[0:CHUNK]
        for hd in range(1, heads_b):
            s = jnp.where(lane_head == hd, r[hd * CHUNK:(hd + 1) * CHUNK], s)
        mix_ref[c0:c0 + CHUNK, d_a:d_a + d_b] = (u * (s + bsf_ref[...])).astype(bf16)

    for c in range(cext_ref.shape[0]):
        lo = c * V7X_LANES
        cext_ref[c, C_HALO:C_HALO + rows, :] = (
            p_ref[:, o_cc + lo:o_cc + lo + V7X_LANES] * p_ref[:, o_cx + lo:o_cx + lo + V7X_LANES])
    _causal_conv(cext_ref, ccw_ref, cconv_ref, C_HALO, CONV_C_WIDTH, rows, None)
    for r0 in range(0, rows, CONV_ROW_BLOCK):
        mix_ref[r0:r0 + CONV_ROW_BLOCK, d_a + d_b:d_a + d_b + d_c] = (
            p_ref[r0:r0 + CONV_ROW_BLOCK, o_cb:o_cb + d_c]
            * _lane_tiles(cconv_ref, r0, r0 + CONV_ROW_BLOCK)).astype(bf16)
    cext_ref[:, 0:C_HALO, :] = cext_ref[:, rows:rows + C_HALO, :]

    o_ref[...] = x_ref[...] + _dot(mix_ref[...], wout_ref[...])


def _mixer(x, norm_g, w_in, conv_a_w, conv_a_b, ln_a_g, ln_a_b, ln_v_g, ln_v_b, w_s, b_s,
           conv_c_w, w_out, seq, cast_next=()):
    n, d = x.shape
    d_a = conv_a_w.shape[1]
    d_b = ln_v_g.shape[0]
    d_c = conv_c_w.shape[1]
    heads_b = d_b // HEAD_DIM
    d_proj = w_in.shape[1]
    assert n % seq == 0 and seq % MIX_ROWS == 0
    assert MIX_ROWS % CHUNK == 0 and MIX_ROWS % CONV_ROW_BLOCK == 0
    assert MIX_ROWS % (CONV_PHASES * CONV_PHASE_ROWS) == 0
    assert d_a % V7X_LANES == 0 and d_c % V7X_LANES == 0
    assert w_s.shape == (heads_b, CHUNK, CHUNK) and d_proj == 2 * d_a + 2 * d_b + 3 * d_c
    bs_full = jnp.repeat(b_s.T, HEAD_DIM, axis=1)
    row_spec = pl.BlockSpec((MIX_ROWS, d), lambda i: (i, 0))
    vec = lambda a: a.reshape(1, -1)
    args = [x, vec(norm_g), w_in, conv_a_w, vec(conv_a_b), vec(ln_a_g), vec(ln_a_b),
            vec(ln_v_g), vec(ln_v_b), w_s, bs_full, conv_c_w, w_out]
    in_specs = [row_spec] + [_resident(a.shape) for a in args[1:]]
    n_steps = n // MIX_ROWS
    casts = [_cast_specs(w, layer, n_steps, lambda i: i) for w, layer in cast_next]
    in_specs += [c[0] for c in casts]
    args += [w for w, _ in cast_next]
    y, *copies = pl.pallas_call(
        functools.partial(_mixer_kernel, d_a=d_a, d_b=d_b, d_c=d_c, n_cast=len(casts),
                          tiles_per_seq=seq // MIX_ROWS),
        out_shape=[jax.ShapeDtypeStruct((n, d), f32)] + [c[2] for c in casts],
        grid=(n_steps,),
        in_specs=in_specs,
        out_specs=[row_spec] + [c[1] for c in casts],
        scratch_shapes=[
            pltpu.VMEM((MIX_ROWS, d_proj), f32),
            pltpu.VMEM((d_a // V7X_LANES, A_HALO + MIX_ROWS, V7X_LANES), f32),
            pltpu.VMEM((d_c // V7X_LANES, C_HALO + MIX_ROWS, V7X_LANES), f32),
            pltpu.VMEM((d_a // V7X_LANES, MIX_ROWS, V7X_LANES), f32),
            pltpu.VMEM((d_c // V7X_LANES, MIX_ROWS, V7X_LANES), f32),
            pltpu.VMEM((MIX_ROWS, d_a + d_b + d_c), bf16),
            pltpu.VMEM((heads_b * CHUNK, CHUNK), bf16),
        ],
        compiler_params=pltpu.CompilerParams(
            dimension_semantics=("arbitrary",), vmem_limit_bytes=V7X_VMEM_LIMIT_BYTES),
        name="mixer",
    )(*args)
    return y, copies


def kernel(x, ffn1_norm, ffn1_w_gate, ffn1_w_up, ffn1_w_down, mix_norm, w_in, conv_a_w, conv_a_b,
           ln_a_g, ln_a_b, ln_v_g, ln_v_b, w_s, b_s, conv_c_w, w_out, ffn2_norm, ffn2_w_gate,
           ffn2_w_up, ffn2_w_down, final_norm):
    bsz, seq, d = x.shape
    depth = w_in.shape[0]
    xf = x.reshape(bsz * seq, d)
    ffn1_w = (ffn1_w_gate, ffn1_w_up, ffn1_w_down)
    ffn2_w = (ffn2_w_gate, ffn2_w_up, ffn2_w_down)
    w1 = [w[0].astype(bf16) for w in ffn1_w]
    for l in range(depth):
        xf, (w_in_bf, w_out_bf) = _ffn(xf, ffn1_norm[l], *w1, cast_next=((w_in, l), (w_out, l)))
        xf, w2 = _mixer(xf, mix_norm[l], w_in_bf, conv_a_w[l], conv_a_b[l], ln_a_g[l], ln_a_b[l],
                        ln_v_g[l], ln_v_b[l], w_s[l], b_s[l], conv_c_w[l], w_out_bf, seq=seq,
                        cast_next=[(w, l) for w in ffn2_w])
        last = l == depth - 1
        xf, w1 = _ffn(xf, ffn2_norm[l], *w2,
                      final_g=final_norm if last else None,
                      cast_next=() if last else [(w, l + 1) for w in ffn1_w])
    return xf.reshape(bsz, seq, d)
```

```python
import functools

import jax
import jax.numpy as jnp
from jax import lax
from jax.experimental import pallas as pl
from jax.experimental.pallas import tpu as pltpu

RMS_EPS = 1e-6
LN_EPS = 1e-5

HEAD_DIM = 64
CHUNK = 128
CONV_A_WIDTH = 31
CONV_C_WIDTH = 3

V7X_LANES = 128
V7X_BF16_SUBLANES = 16
V7X_MXU_COLS = 256
V7X_VMEM_LIMIT_BYTES = 56 * 1024 * 1024

FFN_ROWS = 1024
MIX_ROWS = 1024
A_HALO = 32
C_HALO = 8
CONV_ROW_BLOCK = 64
CONV_PHASES = 2
CONV_PHASE_ROWS = 128

f32 = jnp.float32
bf16 = jnp.bfloat16


def _rms_norm(x, g):
    return x * lax.rsqrt(jnp.mean(x * x, axis=-1, keepdims=True) + RMS_EPS) * g


def _layer_norm(x, g, b):
    mu = jnp.mean(x, axis=-1, keepdims=True)
    xc = x - mu
    var = jnp.mean(xc * xc, axis=-1, keepdims=True)
    return xc * lax.rsqrt(var + LN_EPS) * g + b


def _dot(a, b):
    return jnp.dot(a, b, preferred_element_type=f32)


def _cast_slabs(src_refs, dst_refs):
    for src, dst in zip(src_refs, dst_refs):
        dst[...] = src[...].astype(bf16)


def _ffn_kernel(*refs, ff_chunks, final, n_cast):
    x_ref, g_ref, wg_ref, wu_ref, wd_ref = refs[:5]
    rest = refs[5:]
    if final:
        fn_ref, rest = rest[0], rest[1:]
    cast_src, o_ref, cast_dst = rest[:n_cast], rest[n_cast], rest[n_cast + 1:2 * n_cast + 1]
    h_ref, acc_ref = rest[2 * n_cast + 1:]
    _cast_slabs(cast_src, cast_dst)
    h_ref[...] = _rms_norm(x_ref[...], g_ref[...]).astype(bf16)
    start = 0
    for width in ff_chunks:
        cols = slice(start, start + width)
        h = h_ref[...]
        gate = _dot(h, wg_ref[:, cols])
        up = _dot(h, wu_ref[:, cols])
        act = (gate * jax.nn.sigmoid(gate) * up).astype(bf16)
        down = _dot(act, wd_ref[cols, :])
        if start == 0:
            acc_ref[...] = down
        else:
            acc_ref[...] += down
        start += width
    y = x_ref[...] + 0.5 * acc_ref[...]
    if final:
        y = _rms_norm(y, fn_ref[...])
    o_ref[...] = y


def _ff_chunks(d_ff):
    assert d_ff % V7X_MXU_COLS == 0
    return (V7X_MXU_COLS,) * (d_ff // V7X_MXU_COLS)


def _resident(shape):
    return pl.BlockSpec(shape, lambda *_: (0,) * len(shape), pipeline_mode=pl.Buffered(1))


def _cast_specs(stacked, layer, n_steps, step_of):
    _, rows, cols = stacked.shape
    n_slabs = n_steps
    while rows % n_slabs or (rows // n_slabs) % V7X_BF16_SUBLANES:
        assert n_slabs % 2 == 0
        n_slabs //= 2
    slab = rows // n_slabs

    def slab_of(*g):
        return jnp.minimum(step_of(*g), n_slabs - 1)

    return (pl.BlockSpec((None, slab, cols), lambda *g: (layer, slab_of(*g), 0)),
            pl.BlockSpec((slab, cols), lambda *g: (slab_of(*g), 0)),
            jax.ShapeDtypeStruct((rows, cols), bf16))


def _ffn(x, norm_g, w_gate, w_up, w_down, final_g=None, cast_next=()):
    n, d = x.shape
    d_ff = w_gate.shape[1]
    assert n % FFN_ROWS == 0
    final = final_g is not None
    n_steps = n // FFN_ROWS
    row_spec = pl.BlockSpec((FFN_ROWS, d), lambda i: (i, 0))
    in_specs = [row_spec, _resident((1, d)), _resident(w_gate.shape), _resident(w_up.shape),
                _resident(w_down.shape)]
    args = [x, norm_g.reshape(1, d), w_gate, w_up, w_down]
    if final:
        in_specs.append(_resident((1, d)))
        args.append(final_g.reshape(1, d))
    casts = [_cast_specs(w, layer, n_steps, lambda i: i) for w, layer in cast_next]
    in_specs += [c[0] for c in casts]
    args += [w for w, _ in cast_next]
    y, *copies = pl.pallas_call(
        functools.partial(_ffn_kernel, ff_chunks=_ff_chunks(d_ff), final=final,
                          n_cast=len(casts)),
        out_shape=[jax.ShapeDtypeStruct((n, d), f32)] + [c[2] for c in casts],
        grid=(n_steps,),
        in_specs=in_specs,
        out_specs=[row_spec] + [c[1] for c in casts],
        scratch_shapes=[pltpu.VMEM((FFN_ROWS, d), bf16), pltpu.VMEM((FFN_ROWS, d), f32)],
        compiler_params=pltpu.CompilerParams(
            dimension_semantics=("arbitrary",), vmem_limit_bytes=V7X_VMEM_LIMIT_BYTES),
        name="ffn_final" if final else "ffn",
    )(*args)
    return y, copies


def _causal_conv(ext_ref, w_ref, out_ref, halo, width, rows, bias_ref):
    n = CONV_PHASE_ROWS
    span = n * CONV_PHASES
    first = halo - (width - 1)
    for c in range(ext_ref.shape[0]):
        lanes = slice(c * V7X_LANES, (c + 1) * V7X_LANES)
        for t0 in range(0, rows, span):
            for m in range(CONV_PHASES):
                acc = None if bias_ref is None else bias_ref[:, lanes]
                for k in range(width):
                    src = first + t0 + m + k
                    term = w_ref[k:k + 1, lanes] * ext_ref[c, pl.ds(src, n, stride=CONV_PHASES), :]
                    acc = term if acc is None else acc + term
                out_ref[c, pl.ds(t0 + m, n, stride=CONV_PHASES), :] = acc


def _lane_tiles(ref, r0, r1):
    return jnp.concatenate([ref[c, r0:r1, :] for c in range(ref.shape[0])], axis=1)


def _mixer_kernel(*refs, d_a, d_b, d_c, n_cast):
    (x_ref, g_ref, win_ref, caw_ref, cab_ref, lag_ref, lab_ref, lvg_ref, lvb_ref,
     ws_ref, bsf_ref, ccw_ref, wout_ref) = refs[:13]
    rest = refs[13:]
    cast_src, o_ref, cast_dst = rest[:n_cast], rest[n_cast], rest[n_cast + 1:2 * n_cast + 1]
    p_ref, aext_ref, cext_ref, aconv_ref, cconv_ref, mix_ref, wsm_ref = rest[2 * n_cast + 1:]
    _cast_slabs(cast_src, cast_dst)
    rows = x_ref.shape[1]
    heads_b = d_b // HEAD_DIM

    @pl.when(pl.program_id(1) == 0)
    def _():
        aext_ref[:, 0:A_HALO, :] = jnp.zeros((aext_ref.shape[0], A_HALO, V7X_LANES), f32)
        cext_ref[:, 0:C_HALO, :] = jnp.zeros((cext_ref.shape[0], C_HALO, V7X_LANES), f32)
        t_idx = lax.broadcasted_iota(jnp.int32, (CHUNK, CHUNK), 0)
        s_idx = lax.broadcasted_iota(jnp.int32, (CHUNK, CHUNK), 1)
        for h in range(heads_b):
            wsm_ref[h * CHUNK:(h + 1) * CHUNK, :] = jnp.where(
                s_idx <= t_idx, ws_ref[h], 0.0).astype(bf16)

    h = _rms_norm(x_ref[0], g_ref[...]).astype(bf16)
    p_ref[...] = _dot(h, win_ref[...])

    o_av, o_ag, o_bu, o_bv = 0, d_a, 2 * d_a, 2 * d_a + d_b
    o_cb = 2 * d_a + 2 * d_b
    o_cc, o_cx = o_cb + d_c, o_cb + 2 * d_c

    for c in range(aext_ref.shape[0]):
        lo = c * V7X_LANES
        aext_ref[c, A_HALO:A_HALO + rows, :] = (
            p_ref[:, o_av + lo:o_av + lo + V7X_LANES]
            * jax.nn.sigmoid(p_ref[:, o_ag + lo:o_ag + lo + V7X_LANES]))
    _causal_conv(aext_ref, caw_ref, aconv_ref, A_HALO, CONV_A_WIDTH, rows, cab_ref)
    for r0 in range(0, rows, CONV_ROW_BLOCK):
        y = _layer_norm(_lane_tiles(aconv_ref, r0, r0 + CONV_ROW_BLOCK), lag_ref[...], lab_ref[...])
        mix_ref[r0:r0 + CONV_ROW_BLOCK, 0:d_a] = (y * jax.nn.sigmoid(y)).astype(bf16)
    aext_ref[:, 0:A_HALO, :] = aext_ref[:, rows:rows + A_HALO, :]

    lane_head = lax.broadcasted_iota(jnp.int32, (CHUNK, d_b), 1) // HEAD_DIM
    for c0 in range(0, rows, CHUNK):
        u = jax.nn.gelu(p_ref[c0:c0 + CHUNK, o_bu:o_bu + d_b])
        v = jax.nn.gelu(p_ref[c0:c0 + CHUNK, o_bv:o_bv + d_b])
        v = _layer_norm(v, lvg_ref[...], lvb_ref[...]).astype(bf16)
        r = _dot(wsm_ref[...], v)
        s = r[0:CHUNK]
        for hd in range(1, heads_b):
            s = jnp.where(lane_head == hd, r[hd * CHUNK:(hd + 1) * CHUNK], s)
        mix_ref[c0:c0 + CHUNK, d_a:d_a + d_b] = (u * (s + bsf_ref[...])).astype(bf16)

    for c in range(cext_ref.shape[0]):
        lo = c * V7X_LANES
        cext_ref[c, C_HALO:C_HALO + rows, :] = (
            p_ref[:, o_cc + lo:o_cc + lo + V7X_LANES] * p_ref[:, o_cx + lo:o_cx + lo + V7X_LANES])
    _causal_conv(cext_ref, ccw_ref, cconv_ref, C_HALO, CONV_C_WIDTH, rows, None)
    for r0 in range(0, rows, CONV_ROW_BLOCK):
        mix_ref[r0:r0 + CONV_ROW_BLOCK, d_a + d_b:d_a + d_b + d_c] = (
            p_ref[r0:r0 + CONV_ROW_BLOCK, o_cb:o_cb + d_c]
            * _lane_tiles(cconv_ref, r0, r0 + CONV_ROW_BLOCK)).astype(bf16)
    cext_ref[:, 0:C_HALO, :] = cext_ref[:, rows:rows + C_HALO, :]

    o_ref[0] = x_ref[0] + _dot(mix_ref[...], wout_ref[...])


def _mixer(x, norm_g, w_in, conv_a_w, conv_a_b, ln_a_g, ln_a_b, ln_v_g, ln_v_b, w_s, b_s,
           conv_c_w, w_out, cast_next=()):
    bsz, seq, d = x.shape
    d_a = conv_a_w.shape[1]
    d_b = ln_v_g.shape[0]
    d_c = conv_c_w.shape[1]
    heads_b = d_b // HEAD_DIM
    d_proj = w_in.shape[1]
    assert seq % MIX_ROWS == 0 and MIX_ROWS % CHUNK == 0 and MIX_ROWS % CONV_ROW_BLOCK == 0
    assert MIX_ROWS % (CONV_PHASES * CONV_PHASE_ROWS) == 0
    assert d_a % V7X_LANES == 0 and d_c % V7X_LANES == 0
    assert w_s.shape == (heads_b, CHUNK, CHUNK) and d_proj == 2 * d_a + 2 * d_b + 3 * d_c
    bs_full = jnp.repeat(b_s.T, HEAD_DIM, axis=1)
    row_spec = pl.BlockSpec((1, MIX_ROWS, d), lambda b, t: (b, t, 0))
    vec = lambda a: a.reshape(1, -1)
    args = [x, vec(norm_g), w_in, conv_a_w, vec(conv_a_b), vec(ln_a_g), vec(ln_a_b),
            vec(ln_v_g), vec(ln_v_b), w_s, bs_full, conv_c_w, w_out]
    in_specs = [row_spec] + [_resident(a.shape) for a in args[1:]]
    tiles = seq // MIX_ROWS
    casts = [_cast_specs(w, layer, bsz * tiles, lambda b, t: b * tiles + t)
             for w, layer in cast_next]
    in_specs += [c[0] for c in casts]
    args += [w for w, _ in cast_next]
    y, *copies = pl.pallas_call(
        functools.partial(_mixer_kernel, d_a=d_a, d_b=d_b, d_c=d_c, n_cast=len(casts)),
        out_shape=[jax.ShapeDtypeStruct((bsz, seq, d), f32)] + [c[2] for c in casts],
        grid=(bsz, tiles),
        in_specs=in_specs,
        out_specs=[row_spec] + [c[1] for c in casts],
        scratch_shapes=[
            pltpu.VMEM((MIX_ROWS, d_proj), f32),
            pltpu.VMEM((d_a // V7X_LANES, A_HALO + MIX_ROWS, V7X_LANES), f32),
            pltpu.VMEM((d_c // V7X_LANES, C_HALO + MIX_ROWS, V7X_LANES), f32),
            pltpu.VMEM((d_a // V7X_LANES, MIX_ROWS, V7X_LANES), f32),
            pltpu.VMEM((d_c // V7X_LANES, MIX_ROWS, V7X_LANES), f32),
            pltpu.VMEM((MIX_ROWS, d_a + d_b + d_c), bf16),
            pltpu.VMEM((heads_b * CHUNK, CHUNK), bf16),
        ],
        compiler_params=pltpu.CompilerParams(
            dimension_semantics=("arbitrary", "arbitrary"),
            vmem_limit_bytes=V7X_VMEM_LIMIT_BYTES),
        name="mixer",
    )(*args)
    return y, copies


def kernel(x, ffn1_norm, ffn1_w_gate, ffn1_w_up, ffn1_w_down, mix_norm, w_in, conv_a_w, conv_a_b,
           ln_a_g, ln_a_b, ln_v_g, ln_v_b, w_s, b_s, conv_c_w, w_out, ffn2_norm, ffn2_w_gate,
           ffn2_w_up, ffn2_w_down, final_norm):
    bsz, seq, d = x.shape
    depth = w_in.shape[0]
    xf = x.reshape(bsz * seq, d)
    ffn1_w = (ffn1_w_gate, ffn1_w_up, ffn1_w_down)
    ffn2_w = (ffn2_w_gate, ffn2_w_up, ffn2_w_down)
    w1 = [w[0].astype(bf16) for w in ffn1_w]
    for l in range(depth):
        xf, (w_in_bf, w_out_bf) = _ffn(xf, ffn1_norm[l], *w1, cast_next=((w_in, l), (w_out, l)))
        xm, w2 = _mixer(xf.reshape(bsz, seq, d), mix_norm[l], w_in_bf, conv_a_w[l], conv_a_b[l],
                        ln_a_g[l], ln_a_b[l], ln_v_g[l], ln_v_b[l], w_s[l], b_s[l], conv_c_w[l],
                        w_out_bf, cast_next=[(w, l) for w in ffn2_w])
        last = l == depth - 1
        xf, w1 = _ffn(xm.reshape(bsz * seq, d), ffn2_norm[l], *w2,
                      final_g=final_norm if last else None,
                      cast_next=() if last else [(w, l + 1) for w in ffn1_w])
    return xf.reshape(bsz, seq, d)
```

```python
import functools

import jax
import jax.numpy as jnp
from jax import lax
from jax.experimental import pallas as pl
from jax.experimental.pallas import tpu as pltpu

RMS_EPS = 1e-6
LN_EPS = 1e-5

HEAD_DIM = 64
CHUNK = 128
CONV_A_WIDTH = 31
CONV_C_WIDTH = 3

V7X_LANES = 128
V7X_BF16_SUBLANES = 16
V7X_MXU_COLS = 256
V7X_VMEM_LIMIT_BYTES = 56 * 1024 * 1024

FFN_ROWS = 1024
MIX_ROWS = 1024
A_HALO = 32
C_HALO = 8
CONV_ROW_BLOCK = 64
CONV_PHASES = 2
CONV_PHASE_ROWS = 64

f32 = jnp.float32
bf16 = jnp.bfloat16


def _rms_norm(x, g):
    return x * lax.rsqrt(jnp.mean(x * x, axis=-1, keepdims=True) + RMS_EPS) * g


def _layer_norm(x, g, b):
    mu = jnp.mean(x, axis=-1, keepdims=True)
    xc = x - mu
    var = jnp.mean(xc * xc, axis=-1, keepdims=True)
    return xc * lax.rsqrt(var + LN_EPS) * g + b


def _dot(a, b):
    return jnp.dot(a, b, preferred_element_type=f32)


def _cast_slabs(src_refs, dst_refs):
    for src, dst in zip(src_refs, dst_refs):
        dst[...] = src[...].astype(bf16)


def _ffn_kernel(*refs, ff_chunks, final, n_cast):
    x_ref, g_ref, wg_ref, wu_ref, wd_ref = refs[:5]
    rest = refs[5:]
    if final:
        fn_ref, rest = rest[0], rest[1:]
    cast_src, o_ref, cast_dst = rest[:n_cast], rest[n_cast], rest[n_cast + 1:2 * n_cast + 1]
    h_ref, acc_ref = rest[2 * n_cast + 1:]
    _cast_slabs(cast_src, cast_dst)
    h_ref[...] = _rms_norm(x_ref[...], g_ref[...]).astype(bf16)
    start = 0
    for width in ff_chunks:
        cols = slice(start, start + width)
        h = h_ref[...]
        gate = _dot(h, wg_ref[:, cols])
        up = _dot(h, wu_ref[:, cols])
        act = (gate * jax.nn.sigmoid(gate) * up).astype(bf16)
        down = _dot(act, wd_ref[cols, :])
        if start == 0:
            acc_ref[...] = down
        else:
            acc_ref[...] += down
        start += width
    y = x_ref[...] + 0.5 * acc_ref[...]
    if final:
        y = _rms_norm(y, fn_ref[...])
    o_ref[...] = y


def _ff_chunks(d_ff):
    assert d_ff % V7X_MXU_COLS == 0
    return (V7X_MXU_COLS,) * (d_ff // V7X_MXU_COLS)


def _resident(shape):
    return pl.BlockSpec(shape, lambda *_: (0,) * len(shape), pipeline_mode=pl.Buffered(1))


def _cast_specs(stacked, layer, n_steps, step_of):
    _, rows, cols = stacked.shape
    n_slabs = n_steps
    while rows % n_slabs or (rows // n_slabs) % V7X_BF16_SUBLANES:
        assert n_slabs % 2 == 0
        n_slabs //= 2
    slab = rows // n_slabs

    def slab_of(*g):
        return jnp.minimum(step_of(*g), n_slabs - 1)

    return (pl.BlockSpec((None, slab, cols), lambda *g: (layer, slab_of(*g), 0)),
            pl.BlockSpec((slab, cols), lambda *g: (slab_of(*g), 0)),
            jax.ShapeDtypeStruct((rows, cols), bf16))


def _ffn(x, norm_g, w_gate, w_up, w_down, final_g=None, cast_next=()):
    n, d = x.shape
    d_ff = w_gate.shape[1]
    assert n % FFN_ROWS == 0
    final = final_g is not None
    n_steps = n // FFN_ROWS
    row_spec = pl.BlockSpec((FFN_ROWS, d), lambda i: (i, 0))
    in_specs = [row_spec, _resident((1, d)), _resident(w_gate.shape), _resident(w_up.shape),
                _resident(w_down.shape)]
    args = [x, norm_g.reshape(1, d), w_gate, w_up, w_down]
    if final:
        in_specs.append(_resident((1, d)))
        args.append(final_g.reshape(1, d))
    casts = [_cast_specs(w, layer, n_steps, lambda i: i) for w, layer in cast_next]
    in_specs += [c[0] for c in casts]
    args += [w for w, _ in cast_next]
    y, *copies = pl.pallas_call(
        functools.partial(_ffn_kernel, ff_chunks=_ff_chunks(d_ff), final=final,
                          n_cast=len(casts)),
        out_shape=[jax.ShapeDtypeStruct((n, d), f32)] + [c[2] for c in casts],
        grid=(n_steps,),
        in_specs=in_specs,
        out_specs=[row_spec] + [c[1] for c in casts],
        scratch_shapes=[pltpu.VMEM((FFN_ROWS, d), bf16), pltpu.VMEM((FFN_ROWS, d), f32)],
        compiler_params=pltpu.CompilerParams(
            dimension_semantics=("arbitrary",), vmem_limit_bytes=V7X_VMEM_LIMIT_BYTES),
        name="ffn_final" if final else "ffn",
    )(*args)
    return y, copies


def _causal_conv(ext_ref, w_ref, out_ref, halo, width, rows, bias_ref):
    n = CONV_PHASE_ROWS
    span = n * CONV_PHASES
    first = halo - (width - 1)
    for c in range(ext_ref.shape[0]):
        lanes = slice(c * V7X_LANES, (c + 1) * V7X_LANES)
        for t0 in range(0, rows, span):
            for m in range(CONV_PHASES):
                acc = None if bias_ref is None else bias_ref[:, lanes]
                for k in range(width):
                    src = first + t0 + m + k
                    term = w_ref[k:k + 1, lanes] * ext_ref[c, pl.ds(src, n, stride=CONV_PHASES), :]
                    acc = term if acc is None else acc + term
                out_ref[c, pl.ds(t0 + m, n, stride=CONV_PHASES), :] = acc


def _lane_tiles(ref, r0, r1):
    return jnp.concatenate([ref[c, r0:r1, :] for c in range(ref.shape[0])], axis=1)


def _mixer_kernel(*refs, d_a, d_b, d_c, n_cast):
    (x_ref, g_ref, win_ref, caw_ref, cab_ref, lag_ref, lab_ref, lvg_ref, lvb_ref,
     ws_ref, bsf_ref, ccw_ref, wout_ref) = refs[:13]
    rest = refs[13:]
    cast_src, o_ref, cast_dst = rest[:n_cast], rest[n_cast], rest[n_cast + 1:2 * n_cast + 1]
    p_ref, aext_ref, cext_ref, aconv_ref, cconv_ref, mix_ref, wsm_ref = rest[2 * n_cast + 1:]
    _cast_slabs(cast_src, cast_dst)
    rows = x_ref.shape[1]
    heads_b = d_b // HEAD_DIM

    @pl.when(pl.program_id(1) == 0)
    def _():
        aext_ref[:, 0:A_HALO, :] = jnp.zeros((aext_ref.shape[0], A_HALO, V7X_LANES), f32)
        cext_ref[:, 0:C_HALO, :] = jnp.zeros((cext_ref.shape[0], C_HALO, V7X_LANES), f32)
        t_idx = lax.broadcasted_iota(jnp.int32, (CHUNK, CHUNK), 0)
        s_idx = lax.broadcasted_iota(jnp.int32, (CHUNK, CHUNK), 1)
        for h in range(heads_b):
            wsm_ref[h * CHUNK:(h + 1) * CHUNK, :] = jnp.where(
                s_idx <= t_idx, ws_ref[h], 0.0).astype(bf16)

    h = _rms_norm(x_ref[0], g_ref[...]).astype(bf16)
    p_ref[...] = _dot(h, win_ref[...])

    o_av, o_ag, o_bu, o_bv = 0, d_a, 2 * d_a, 2 * d_a + d_b
    o_cb = 2 * d_a + 2 * d_b
    o_cc, o_cx = o_cb + d_c, o_cb + 2 * d_c

    for c in range(aext_ref.shape[0]):
        lo = c * V7X_LANES
        aext_ref[c, A_HALO:A_HALO + rows, :] = (
            p_ref[:, o_av + lo:o_av + lo + V7X_LANES]
            * jax.nn.sigmoid(p_ref[:, o_ag + lo:o_ag + lo + V7X_LANES]))
    _causal_conv(aext_ref, caw_ref, aconv_ref, A_HALO, CONV_A_WIDTH, rows, cab_ref)
    for r0 in range(0, rows, CONV_ROW_BLOCK):
        y = _layer_norm(_lane_tiles(aconv_ref, r0, r0 + CONV_ROW_BLOCK), lag_ref[...], lab_ref[...])
        mix_ref[r0:r0 + CONV_ROW_BLOCK, 0:d_a] = (y * jax.nn.sigmoid(y)).astype(bf16)
    aext_ref[:, 0:A_HALO, :] = aext_ref[:, rows:rows + A_HALO, :]

    lane_head = lax.broadcasted_iota(jnp.int32, (CHUNK, d_b), 1) // HEAD_DIM
    for c0 in range(0, rows, CHUNK):
        u = jax.nn.gelu(p_ref[c0:c0 + CHUNK, o_bu:o_bu + d_b])
        v = jax.nn.gelu(p_ref[c0:c0 + CHUNK, o_bv:o_bv + d_b])
        v = _layer_norm(v, lvg_ref[...], lvb_ref[...]).astype(bf16)
        r = _dot(wsm_ref[...], v)
        s = r[0:CHUNK]
        for hd in range(1, heads_b):
            s = jnp.where(lane_head == hd, r[hd * CHUNK:(hd + 1) * CHUNK], s)
        mix_ref[c0:c0 + CHUNK, d_a:d_a + d_b] = (u * (s + bsf_ref[...])).astype(bf16)

    for c in range(cext_ref.shape[0]):
        lo = c * V7X_LANES
        cext_ref[c, C_HALO:C_HALO + rows, :] = (
            p_ref[:, o_cc + lo:o_cc + lo + V7X_LANES] * p_ref[:, o_cx + lo:o_cx + lo + V7X_LANES])
    _causal_conv(cext_ref, ccw_ref, cconv_ref, C_HALO, CONV_C_WIDTH, rows, None)
    for r0 in range(0, rows, CONV_ROW_BLOCK):
        mix_ref[r0:r0 + CONV_ROW_BLOCK, d_a + d_b:d_a + d_b + d_c] = (
            p_ref[r0:r0 + CONV_ROW_BLOCK, o_cb:o_cb + d_c]
            * _lane_tiles(cconv_ref, r0, r0 + CONV_ROW_BLOCK)).astype(bf16)
    cext_ref[:, 0:C_HALO, :] = cext_ref[:, rows:rows + C_HALO, :]

    part = x_ref[0] + _dot(mix_ref[:, d_a:], wout_ref[d_a:, :])
    o_ref[0] = part + _dot(mix_ref[:, 0:d_a], wout_ref[0:d_a, :])


def _mixer(x, norm_g, w_in, conv_a_w, conv_a_b, ln_a_g, ln_a_b, ln_v_g, ln_v_b, w_s, b_s,
           conv_c_w, w_out, cast_next=()):
    bsz, seq, d = x.shape
    d_a = conv_a_w.shape[1]
    d_b = ln_v_g.shape[0]
    d_c = conv_c_w.shape[1]
    heads_b = d_b // HEAD_DIM
    d_proj = w_in.shape[1]
    assert seq % MIX_ROWS == 0 and MIX_ROWS % CHUNK == 0 and MIX_ROWS % CONV_ROW_BLOCK == 0
    assert MIX_ROWS % (CONV_PHASES * CONV_PHASE_ROWS) == 0
    assert d_a % V7X_LANES == 0 and d_c % V7X_LANES == 0
    assert w_s.shape == (heads_b, CHUNK, CHUNK) and d_proj == 2 * d_a + 2 * d_b + 3 * d_c
    bs_full = jnp.repeat(b_s.T, HEAD_DIM, axis=1)
    row_spec = pl.BlockSpec((1, MIX_ROWS, d), lambda b, t: (b, t, 0))
    vec = lambda a: a.reshape(1, -1)
    args = [x, vec(norm_g), w_in, conv_a_w, vec(conv_a_b), vec(ln_a_g), vec(ln_a_b),
            vec(ln_v_g), vec(ln_v_b), w_s, bs_full, conv_c_w, w_out]
    in_specs = [row_spec] + [_resident(a.shape) for a in args[1:]]
    tiles = seq // MIX_ROWS
    casts = [_cast_specs(w, layer, bsz * tiles, lambda b, t: b * tiles + t)
             for w, layer in cast_next]
    in_specs += [c[0] for c in casts]
    args += [w for w, _ in cast_next]
    y, *copies = pl.pallas_call(
        functools.partial(_mixer_kernel, d_a=d_a, d_b=d_b, d_c=d_c, n_cast=len(casts)),
        out_shape=[jax.ShapeDtypeStruct((bsz, seq, d), f32)] + [c[2] for c in casts],
        grid=(bsz, tiles),
        in_specs=in_specs,
        out_specs=[row_spec] + [c[1] for c in casts],
        scratch_shapes=[
            pltpu.VMEM((MIX_ROWS, d_proj), f32),
            pltpu.VMEM((d_a // V7X_LANES, A_HALO + MIX_ROWS, V7X_LANES), f32),
            pltpu.VMEM((d_c // V7X_LANES, C_HALO + MIX_ROWS, V7X_LANES), f32),
            pltpu.VMEM((d_a // V7X_LANES, MIX_ROWS, V7X_LANES), f32),
            pltpu.VMEM((d_c // V7X_LANES, MIX_ROWS, V7X_LANES), f32),
            pltpu.VMEM((MIX_ROWS, d_a + d_b + d_c), bf16),
            pltpu.VMEM((heads_b * CHUNK, CHUNK), bf16),
        ],
        compiler_params=pltpu.CompilerParams(
            dimension_semantics=("arbitrary", "arbitrary"),
            vmem_limit_bytes=V7X_VMEM_LIMIT_BYTES),
        name="mixer",
    )(*args)
    return y, copies


def kernel(x, ffn1_norm, ffn1_w_gate, ffn1_w_up, ffn1_w_down, mix_norm, w_in, conv_a_w, conv_a_b,
           ln_a_g, ln_a_b, ln_v_g, ln_v_b, w_s, b_s, conv_c_w, w_out, ffn2_norm, ffn2_w_gate,
           ffn2_w_up, ffn2_w_down, final_norm):
    bsz, seq, d = x.shape
    depth = w_in.shape[0]
    xf = x.reshape(bsz * seq, d)
    ffn1_w = (ffn1_w_gate, ffn1_w_up, ffn1_w_down)
    ffn2_w = (ffn2_w_gate, ffn2_w_up, ffn2_w_down)
    w1 = [w[0].astype(bf16) for w in ffn1_w]
    for l in range(depth):
        xf, (w_in_bf, w_out_bf) = _ffn(xf, ffn1_norm[l], *w1, cast_next=((w_in, l), (w_out, l)))
        xm, w2 = _mixer(xf.reshape(bsz, seq, d), mix_norm[l], w_in_bf, conv_a_w[l], conv_a_b[l],
                        ln_a_g[l], ln_a_b[l], ln_v_g[l], ln_v_b[l], w_s[l], b_s[l], conv_c_w[l],
                        w_out_bf, cast_next=[(w, l) for w in ffn2_w])
        last = l == depth - 1
        xf, w1 = _ffn(xm.reshape(bsz * seq, d), ffn2_norm[l], *w2,
                      final_g=final_norm if last else None,
                      cast_next=() if last else [(w, l + 1) for w in ffn1_w])
    return xf.reshape(bsz, seq, d)
```
